```python
import jax, jax.numpy as jnp
from jax import lax
import numpy as np

D_MODEL = 1024
BATCH = 16
SEQ = 4096
DEPTH = 1
DEC_BATCH = 128
DEC_SEQ = 4
PAST_LEN = 8192
PAGE_SIZE = 128

N_HEADS_A = 8
N_KV_A = 2
HEAD_DIM_A = (D_MODEL // 2) // N_HEADS_A
ROT_DIM_A = HEAD_DIM_A // 4
ROPE_THETA = 500000.0
N_IDX_HEADS = 8
D_IDX = 64
ROT_DIM_IDX = D_IDX // 4
TOPK_MAX = 256
Q_BLOCK = 128
N_HEADS_R = 4
DV_R = (D_MODEL // 2) // N_HEADS_R
DK_R = DV_R // 2
RET_THETA = 10000.0
RET_CHUNK = 128
D_FF = 4 * D_MODEL
EPS = 1e-6
PROJ_SIZES = (N_HEADS_A * HEAD_DIM_A, N_KV_A * HEAD_DIM_A, N_KV_A * HEAD_DIM_A,
              N_IDX_HEADS * D_IDX, D_IDX, N_IDX_HEADS,
              N_HEADS_R * DK_R, N_HEADS_R * DK_R, N_HEADS_R * DV_R, N_HEADS_R * DV_R)
D_IN_PROJ = sum(PROJ_SIZES)

kernel_name = 'hybrid_dsa_retention_decoder_step'


def _rms_norm(x, g):
    xf = x.astype(jnp.float32)
    y = xf * lax.rsqrt(jnp.mean(xf * xf, axis=-1, keepdims=True) + EPS)
    return (y * g.astype(jnp.float32)).astype(x.dtype)


def _layer_norm(x, g, b):
    xf = x.astype(jnp.float32)
    mu = jnp.mean(xf, axis=-1, keepdims=True)
    var = jnp.mean(jnp.square(xf - mu), axis=-1, keepdims=True)
    y = (xf - mu) * lax.rsqrt(var + EPS)
    return (y * g.astype(jnp.float32) + b.astype(jnp.float32)).astype(x.dtype)


def _rope(x, pos, rot_dim, theta):
    half = rot_dim // 2
    inv = theta ** (-jnp.arange(half, dtype=jnp.float32) / half)
    ang = pos.astype(jnp.float32)[:, None] * inv[None, :]
    cos = jnp.cos(ang)[:, None, :]
    sin = jnp.sin(ang)[:, None, :]
    xr = x[..., :rot_dim].astype(jnp.float32)
    x1, x2 = xr[..., :half], xr[..., half:]
    rot = jnp.concatenate([x1 * cos - x2 * sin, x2 * cos + x1 * sin], axis=-1).astype(x.dtype)
    return jnp.concatenate([rot, x[..., rot_dim:]], axis=-1)


def _project(h, w_in, pos, ln_g, ln_b):
    B, T, _ = h.shape
    z = jnp.einsum('btd,de->bte', h, w_in)
    cuts = np.cumsum(PROJ_SIZES)[:-1].tolist()
    qa, ka, va, iq, ik, iw, rq, rk, rv, rg = jnp.split(z, cuts, axis=-1)
    qa = _rope(qa.reshape(B, T, N_HEADS_A, HEAD_DIM_A), pos, ROT_DIM_A, ROPE_THETA)
    ka = _rope(ka.reshape(B, T, N_KV_A, HEAD_DIM_A), pos, ROT_DIM_A, ROPE_THETA)
    va = va.reshape(B, T, N_KV_A, HEAD_DIM_A)
    iq = _rope(iq.reshape(B, T, N_IDX_HEADS, D_IDX), pos, ROT_DIM_IDX, ROPE_THETA)
    ik = _rope(_layer_norm(ik, ln_g, ln_b)[:, :, None, :], pos, ROT_DIM_IDX, ROPE_THETA)[:, :, 0]
    iw = iw * (N_IDX_HEADS * D_IDX) ** -0.5
    rq = _rope(rq.reshape(B, T, N_HEADS_R, DK_R), pos, DK_R, RET_THETA)
    rk = _rope(rk.reshape(B, T, N_HEADS_R, DK_R) * DK_R ** -0.5, pos, DK_R, RET_THETA)
    rv = rv.reshape(B, T, N_HEADS_R, DV_R)
    return qa, ka, va, iq, ik, iw, rq, rk, rv, rg


def _indexer_scores(iq, iw, ik):
    s = jnp.einsum('bthd,bsd->bths', iq, ik)
    return jnp.einsum('bths,bth->bts', jax.nn.relu(s).astype(jnp.float32), iw.astype(jnp.float32))


def _sparse_attend(q, k_sel, v_sel, valid):
    B, T, H, dh = q.shape
    qg = q.reshape(B, T, N_KV_A, H // N_KV_A, dh)
    logits = jnp.einsum('btgrd,btkgd->btgrk', qg, k_sel).astype(jnp.float32) * dh ** -0.5
    logits = jnp.where(valid[:, :, None, None, :], logits, -jnp.inf)
    p = jax.nn.softmax(logits, axis=-1)
    o = jnp.einsum('btgrk,btkgd->btgrd', p.astype(v_sel.dtype), v_sel)
    return o.reshape(B, T, H * dh)


def _dsa_prompt(qa, ka, va, iq, ik, iw):
    B, S = qa.shape[:2]
    k_top = min(TOPK_MAX, S // 4)
    nb = S // Q_BLOCK
    take_rows = jax.vmap(lambda rows, ids: rows[ids])

    def block(args):
        q_b, iq_b, iw_b, start = args
        qpos = start + jnp.arange(Q_BLOCK)
        sc = _indexer_scores(iq_b, iw_b, ik)
        causal = jnp.arange(S)[None, :] <= qpos[:, None]
        sc = jnp.where(causal[None], sc, -jnp.inf)
        _, idx = lax.top_k(sc, k_top)
        valid = idx <= qpos[None, :, None]
        return _sparse_attend(q_b, take_rows(ka, idx), take_rows(va, idx), valid)

    to_blocks = lambda t: t.reshape(B, nb, Q_BLOCK, *t.shape[2:]).swapaxes(0, 1)
    out = lax.map(block, (to_blocks(qa), to_blocks(iq), to_blocks(iw), jnp.arange(nb) * Q_BLOCK))
    return out.swapaxes(0, 1).reshape(B, S, -1)


def _dsa_sample(qa, ka, va, iq, ik, iw, cache_k, cache_v, cache_kidx, page_table, layer):
    DB, T = qa.shape[:2]
    past = page_table.shape[1] * PAGE_SIZE
    L = past + T
    k_top = min(TOPK_MAX, L // 4)
    ik_past = cache_kidx[layer, page_table].reshape(DB, past, D_IDX)
    ik_all = jnp.concatenate([ik_past, ik.astype(ik_past.dtype)], axis=1)
    qpos = past + jnp.arange(T)
    sc = _indexer_scores(iq, iw, ik_all)
    sc = jnp.where((jnp.arange(L)[None, :] <= qpos[:, None])[None], sc, -jnp.inf)
    _, idx = lax.top_k(sc, k_top)
    valid = idx <= qpos[None, :, None]
    in_past = idx < past
    pidx = jnp.minimum(idx, past - 1)
    page = jnp.take_along_axis(page_table, (pidx // PAGE_SIZE).reshape(DB, -1), axis=1).reshape(idx.shape)
    slot = pidx % PAGE_SIZE
    nidx = jnp.clip(idx - past, 0, T - 1)
    take_new = jax.vmap(lambda rows, ids: rows[ids])
    select = lambda cache, new: jnp.where(in_past[..., None, None],
                                          cache[layer, page, slot].astype(new.dtype),
                                          take_new(new, nidx))
    return _sparse_attend(qa, select(cache_k, ka), select(cache_v, va), valid)


def _retention_log_decay():
    gamma = 1.0 - 2.0 ** (-5.0 - jnp.arange(N_HEADS_R, dtype=jnp.float32))
    return jnp.log(gamma)


def _retention_chunk(q, k, v, state, log_g):
    C = q.shape[1]
    i = jnp.arange(C, dtype=jnp.float32)
    diff = i[:, None] - i[None, :]
    dmask = jnp.where(diff >= 0, jnp.exp(jnp.maximum(diff, 0.0)[None] * log_g[:, None, None]), 0.0)
    qf, kf, vf = q.astype(jnp.float32), k.astype(jnp.float32), v.astype(jnp.float32)
    inner = jnp.einsum('bihd,bjhd->bhij', qf, kf) * dmask[None]
    o = jnp.einsum('bhij,bjhe->bihe', inner, vf)
    cross_decay = jnp.exp((i + 1.0)[:, None] * log_g[None, :])
    o = o + jnp.einsum('bihd,bhde->bihe', qf, state) * cross_decay[None, :, :, None]
    k_decay = jnp.exp((C - 1.0 - i)[:, None] * log_g[None, :])
    new_state = jnp.exp(C * log_g)[None, :, None, None] * state + jnp.einsum('bjhd,bjhe,jh->bhde', kf, vf, k_decay)
    return o, new_state


def _retention_prompt(q, k, v):
    B, S = q.shape[:2]
    nc = S // RET_CHUNK
    log_g = _retention_log_decay()
    to_chunks = lambda t: t.reshape(B, nc, RET_CHUNK, *t.shape[2:]).swapaxes(0, 1)

    def step(st, xs):
        o, st = _retention_chunk(xs[0], xs[1], xs[2], st, log_g)
        return st, o

    init = jnp.zeros((B, N_HEADS_R, DK_R, DV_R), jnp.float32)
    st, o = lax.scan(step, init, (to_chunks(q), to_chunks(k), to_chunks(v)))
    return o.swapaxes(0, 1).reshape(B, S, N_HEADS_R, DV_R), st


def _retention_output(o, gate, gn_g, gn_b, dtype):
    B, T = o.shape[:2]
    mu = jnp.mean(o, axis=-1, keepdims=True)
    var = jnp.mean(jnp.square(o - mu), axis=-1, keepdims=True)
    y = ((o - mu) * lax.rsqrt(var + EPS)).reshape(B, T, N_HEADS_R * DV_R)
    y = y * gn_g.astype(jnp.float32) + gn_b.astype(jnp.float32)
    return (jax.nn.silu(gate.astype(jnp.float32)) * y).astype(dtype)


def _post_mix(x, a_out, r_out, w_out, g_post_mix, g_pre_mlp, w_up, w_down, g_post_mlp):
    mix = jnp.einsum('bte,ed->btd', jnp.concatenate([a_out, r_out], axis=-1), w_out)
    x = x + _rms_norm(mix, g_post_mix)
    u = jax.nn.relu(jnp.einsum('btd,df->btf', _rms_norm(x, g_pre_mlp), w_up))
    return x + _rms_norm(jnp.einsum('btf,fd->btd', u * u, w_down), g_post_mlp)


def setup_inputs(seed: int = 0) -> dict:
    key = jax.random.key(seed)
    ks = jax.random.split(key, 20)
    n_pages = PAST_LEN // PAGE_SIZE
    n_phys = (5 * DEC_BATCH * n_pages) // 4
    nrm = lambda k, shape, scale=1.0: jax.random.normal(k, shape, jnp.float32) * scale
    page_table = jax.random.permutation(ks[6], n_phys)[: DEC_BATCH * n_pages].reshape(DEC_BATCH, n_pages).astype(jnp.int32)
    return {
        'x_prompt': nrm(ks[0], (BATCH, SEQ, D_MODEL)),
        'x_sample': nrm(ks[1], (DEC_BATCH, DEC_SEQ, D_MODEL)),
        'cache_k': nrm(ks[2], (DEPTH, n_phys, PAGE_SIZE, N_KV_A, HEAD_DIM_A)),
        'cache_v': nrm(ks[3], (DEPTH, n_phys, PAGE_SIZE, N_KV_A, HEAD_DIM_A)),
        'cache_kidx': nrm(ks[4], (DEPTH, n_phys, PAGE_SIZE, D_IDX)),
        'state_ret': nrm(ks[5], (DEPTH, DEC_BATCH, N_HEADS_R, DK_R, DV_R), 0.5),
        'page_table': page_table,
        'w_in': nrm(ks[7], (DEPTH, D_MODEL, D_IN_PROJ), D_MODEL ** -0.5),
        'w_out': nrm(ks[8], (DEPTH, D_MODEL, D_MODEL), D_MODEL ** -0.5),
        'w_up': nrm(ks[9], (DEPTH, D_MODEL, D_FF), D_MODEL ** -0.5),
        'w_down': nrm(ks[10], (DEPTH, D_FF, D_MODEL), D_FF ** -0.5),
        'g_pre_mix': 1.0 + nrm(ks[11], (DEPTH, D_MODEL), 0.01),
        'g_post_mix': 1.0 + nrm(ks[12], (DEPTH, D_MODEL), 0.01),
        'g_pre_mlp': 1.0 + nrm(ks[13], (DEPTH, D_MODEL), 0.01),
        'g_post_mlp': 1.0 + nrm(ks[14], (DEPTH, D_MODEL), 0.01),
        'kidx_ln_g': 1.0 + nrm(ks[15], (DEPTH, D_IDX), 0.01),
        'kidx_ln_b': nrm(ks[16], (DEPTH, D_IDX), 0.01),
        'ret_gn_g': 1.0 + nrm(ks[17], (DEPTH, N_HEADS_R * DV_R), 0.01),
        'ret_gn_b': nrm(ks[18], (DEPTH, N_HEADS_R * DV_R), 0.01),
    }


def reference(x_prompt, x_sample, cache_k, cache_v, cache_kidx, state_ret, page_table,
              w_in, w_out, w_up, w_down, g_pre_mix, g_post_mix, g_pre_mlp, g_post_mlp,
              kidx_ln_g, kidx_ln_b, ret_gn_g, ret_gn_b):
    S = x_prompt.shape[1]
    T = x_sample.shape[1]
    past = page_table.shape[1] * PAGE_SIZE
    pos_p = jnp.arange(S)
    pos_s = past + jnp.arange(T)
    log_g = _retention_log_decay()
    xp, xs = x_prompt, x_sample
    kp, vp, ikp, rsp, ks_, vs_, iks, rss = [], [], [], [], [], [], [], []
    for l in range(DEPTH):
        qa, ka, va, iq, ik, iw, rq, rk, rv, rg = _project(_rms_norm(xp, g_pre_mix[l]), w_in[l], pos_p, kidx_ln_g[l], kidx_ln_b[l])
        a_out = _dsa_prompt(qa, ka, va, iq, ik, iw)
        r_o, r_st = _retention_prompt(rq, rk, rv)
        r_out = _retention_output(r_o, rg, ret_gn_g[l], ret_gn_b[l], xp.dtype)
        xp = _post_mix(xp, a_out, r_out, w_out[l], g_post_mix[l], g_pre_mlp[l], w_up[l], w_down[l], g_post_mlp[l])
        kp.append(ka.astype(cache_k.dtype)); vp.append(va.astype(cache_v.dtype))
        ikp.append(ik.astype(cache_kidx.dtype)); rsp.append(r_st.astype(state_ret.dtype))
        qa, ka, va, iq, ik, iw, rq, rk, rv, rg = _project(_rms_norm(xs, g_pre_mix[l]), w_in[l], pos_s, kidx_ln_g[l], kidx_ln_b[l])
        a_out = _dsa_sample(qa, ka, va, iq, ik, iw, cache_k, cache_v, cache_kidx, page_table, l)
        r_o, r_st = _retention_chunk(rq, rk, rv, state_ret[l].astype(jnp.float32), log_g)
        r_out = _retention_output(r_o, rg, ret_gn_g[l], ret_gn_b[l], xs.dtype)
        xs = _post_mix(xs, a_out, r_out, w_out[l], g_post_mix[l], g_pre_mlp[l], w_up[l], w_down[l], g_post_mlp[l])
        ks_.append(ka.astype(cache_k.dtype)); vs_.append(va.astype(cache_v.dtype))
        iks.append(ik.astype(cache_kidx.dtype)); rss.append(r_st.astype(state_ret.dtype))
    return (xp, xs, jnp.stack(kp), jnp.stack(vp), jnp.stack(ikp), jnp.stack(rsp),
            jnp.stack(ks_), jnp.stack(vs_), jnp.stack(iks), jnp.stack(rss))
```

```python
import functools

import numpy as np
import jax
import jax.numpy as jnp
from jax import lax
from jax.experimental import pallas as pl
from jax.experimental.pallas import tpu as pltpu

F32 = jnp.float32
I32 = jnp.int32
MXU_DTYPE = jnp.bfloat16

LANES = 128
N_HEADS_A = 8
N_KV_A = 2
HEAD_DIM_A = 64
ROT_DIM_A = 16
ROPE_THETA = 500000.0
N_IDX_HEADS = 8
D_IDX = 64
TOPK_MAX = 256
N_HEADS_R = 4
DV_R = 128
DK_R = 64
RET_THETA = 10000.0
RET_CHUNK = 128
PAGE_SIZE = 128
EPS = 1e-6
PROJ_SIZES = (N_HEADS_A * HEAD_DIM_A, N_KV_A * HEAD_DIM_A, N_KV_A * HEAD_DIM_A,
              N_IDX_HEADS * D_IDX, D_IDX, N_IDX_HEADS,
              N_HEADS_R * DK_R, N_HEADS_R * DK_R, N_HEADS_R * DV_R, N_HEADS_R * DV_R)

INT_MIN = -(2 ** 31)
NEG_BIAS = -1e30
SAMPLE_ROWS = 16
VMEM_LIMIT = 56 * 1024 * 1024

_SEG = {}
_off = 0
for _name, _w in (("q", 512), ("k", 128), ("v", 128), ("kk", 256), ("vv", 256), ("iq", 512),
                  ("ikw", 128), ("rq", 512), ("rk", 512), ("rv", 512), ("rg", 512)):
    _SEG[_name] = (_off, _off + _w)
    _off += _w
W_PROJ = _off


def _dot(a, b):
    return jnp.dot(a, b, preferred_element_type=F32)


def _dot_nt(a, b):
    return lax.dot_general(a, b, (((1,), (1,)), ((), ())), preferred_element_type=F32)


def _dot_tn(a, b):
    return lax.dot_general(a, b, (((0,), (0,)), ((), ())), preferred_element_type=F32)


def _tile_lanes(t, width):
    reps = width // t.shape[1]
    return t if reps == 1 else jnp.concatenate([t] * reps, axis=1)


def _sortable_key(score):
    score = jnp.where(score == 0.0, 0.0, score)
    bits = pltpu.bitcast(score, I32)
    return jnp.where(bits >= 0, bits, bits ^ 0x7FFFFFFF)


def _rope(z, cos_t, sin_up_t, sin_dn_t, half):
    w = z.shape[1]
    up = pltpu.roll(z, w - half, 1)
    dn = pltpu.roll(z, half, 1)
    return z * _tile_lanes(cos_t, w) + up * _tile_lanes(sin_up_t, w) + dn * _tile_lanes(sin_dn_t, w)


def _proj_kernel(x_ref, tab_ref, g_ref, w_ref, lng_ref, lnb_ref,
                 qa_o, ka_o, va_o, kk_o, vv_o, iq_o, ik_o, ikd_o, iw_o, rq_o, rk_o, rv_o, rg_o):
    x = x_ref[0]
    h = (x * lax.rsqrt(jnp.mean(x * x, axis=-1, keepdims=True) + EPS) * g_ref[...]).astype(MXU_DTYPE)

    def seg(name):
        lo, hi = _SEG[name]
        return _dot(h, w_ref[:, lo:hi])

    tab = tab_ref[...]
    c_a, su_a, sd_a, c_r, su_r, sd_r = [tab[:, i * LANES:(i + 1) * LANES] for i in range(6)]
    half_a = ROT_DIM_A // 2
    half_r = DK_R // 2

    qa_o[0] = _rope(seg("q"), c_a, su_a, sd_a, half_a).astype(qa_o.dtype)
    ka_o[0] = _rope(seg("k"), c_a, su_a, sd_a, half_a)
    va_o[0] = seg("v")
    kk_o[0] = _rope(seg("kk"), c_a, su_a, sd_a, half_a).astype(kk_o.dtype)
    vv_o[0] = seg("vv").astype(vv_o.dtype)
    iq_o[0] = _rope(seg("iq"), c_a, su_a, sd_a, half_a).astype(iq_o.dtype)
    rq_o[0] = _rope(seg("rq"), c_r, su_r, sd_r, half_r).astype(rq_o.dtype)
    rk_o[0] = _rope(seg("rk") * DK_R ** -0.5, c_r, su_r, sd_r, half_r).astype(rk_o.dtype)
    rv_o[0] = seg("rv").astype(rv_o.dtype)
    rg_o[0] = seg("rg")

    zi = seg("ikw")
    lane = lax.broadcasted_iota(I32, zi.shape, 1)
    is_ik = lane < D_IDX
    mu = jnp.sum(jnp.where(is_ik, zi, 0.0), axis=-1, keepdims=True) * (1.0 / D_IDX)
    d = jnp.where(is_ik, zi - mu, 0.0)
    var = jnp.sum(d * d, axis=-1, keepdims=True) * (1.0 / D_IDX)
    y = d * lax.rsqrt(var + EPS) * lng_ref[...] + lnb_ref[...]
    ikr = jnp.where(is_ik, _rope(y, c_a, su_a, sd_a, half_a), 0.0)
    ik_o[0] = ikr[:, :D_IDX]
    ikd_o[0] = (ikr + pltpu.roll(ikr, D_IDX, 1)).astype(ikd_o.dtype)
    iw_o[0] = zi[:, D_IDX:D_IDX + N_IDX_HEADS] * (N_IDX_HEADS * D_IDX) ** -0.5


def _proj_call(x3, tab, g, w, lng, lnb, tm):
    bx, n, d = x3.shape
    grid = (n // tm, bx)
    tok = lambda w_, dt: jax.ShapeDtypeStruct((bx, n, w_), dt)
    tspec = lambda w_: pl.BlockSpec((1, tm, w_), lambda i, b: (b, i, 0))
    const = lambda shape: pl.BlockSpec(shape, lambda i, b: (0,) * len(shape))
    outs = (("qa", 512, MXU_DTYPE), ("ka", 128, F32), ("va", 128, F32), ("kk", 256, MXU_DTYPE),
            ("vv", 256, MXU_DTYPE), ("iq", 512, MXU_DTYPE), ("ik", D_IDX, F32), ("ikd", 128, MXU_DTYPE),
            ("iw", N_IDX_HEADS, F32), ("rq", 512, MXU_DTYPE), ("rk", 512, MXU_DTYPE),
            ("rv", 512, MXU_DTYPE), ("rg", 512, F32))
    res = pl.pallas_call(
        _proj_kernel,
        grid=grid,
        in_specs=[tspec(d),
                  pl.BlockSpec((tm, 6 * LANES), lambda i, b: (i, 0)),
                  const((1, d)), const((d, W_PROJ)), const((1, LANES)), const((1, LANES))],
        out_specs=tuple(tspec(w_) for _, w_, _ in outs),
        out_shape=tuple(tok(w_, dt) for _, w_, dt in outs),
        compiler_params=pltpu.CompilerParams(
            dimension_semantics=("arbitrary", "arbitrary"), vmem_limit_bytes=VMEM_LIMIT),
        name="proj",
    )(x3, tab, g, w, lng, lnb)
    return dict(zip([o[0] for o in outs], res))


def _select_threshold(count_ge, n_valid, k_top):
    def cond(c):
        bit, _, cnt = c
        return jnp.logical_and(bit >= 0, jnp.max(cnt) > k_top)

    def body(c):
        bit, u, cnt = c
        cand = u | lax.shift_left(jnp.int32(1), bit)
        c_new = count_ge(cand ^ INT_MIN)
        ok = c_new >= k_top
        return bit - 1, jnp.where(ok, cand, u), jnp.where(ok, c_new, cnt)

    u0 = jnp.zeros(n_valid.shape, I32)
    _, u, _ = lax.while_loop(cond, body, (jnp.int32(31), u0, n_valid))
    return jnp.maximum(u ^ INT_MIN, INT_MIN + 1)


def _dsa_kernel(qa_ref, iq_ref, iw_ref, ikd_ref, kk_ref, vv_ref, tri_ref, out_ref,
                keys_sc, bias_sc, iqp_sc, qap_sc, iwb_sc, *, tq, k_top):
    tk = tq
    j = pl.program_id(1)
    q0 = j * tq
    nkt = j + 1

    lane = lax.broadcasted_iota(I32, (tq, LANES), 1)
    lo_half = lane < HEAD_DIM_A
    for h in range(N_HEADS_A):
        keep = lo_half if h % 2 == 0 else jnp.logical_not(lo_half)
        sl = slice((h // 2) * LANES, (h // 2 + 1) * LANES)
        iqp_sc[h] = jnp.where(keep, iq_ref[0, :, sl], jnp.zeros((), MXU_DTYPE))
        qap_sc[h] = jnp.where(keep, qa_ref[0, :, sl], jnp.zeros((), MXU_DTYPE))
        iwb_sc[h] = jnp.broadcast_to(iw_ref[0, :, h:h + 1], (tq, LANES))

    row = lax.broadcasted_iota(I32, (tq, tk), 0)
    col = lax.broadcasted_iota(I32, (tq, tk), 1)

    def score_body(kt, carry):
        k0 = pl.multiple_of(kt * tk, tk)
        kb = ikd_ref[0, pl.ds(k0, tk), :]
        acc = jnp.zeros((tq, tk), F32)
        for h in range(N_IDX_HEADS):
            s = _dot_nt(iqp_sc[h], kb)
            acc = acc + jnp.maximum(s, 0.0) * _tile_lanes(iwb_sc[h], tk)
        key = _sortable_key(acc)
        keys_sc[kt] = jnp.where(col + k0 <= row + q0, key, INT_MIN)
        return carry

    lax.fori_loop(0, nkt, score_body, 0)

    def count_where(pred):
        def body(kt, part):
            hit = jnp.where(pred(keys_sc[kt]), 1.0, 0.0)
            for c in range(tk // LANES):
                part = part + hit[:, c * LANES:(c + 1) * LANES]
            return part
        part = lax.fori_loop(0, nkt, body, jnp.zeros((tq, LANES), F32))
        return jnp.sum(part, axis=-1, keepdims=True)

    n_valid = (lax.broadcasted_iota(I32, (tq, 1), 0) + (q0 + 1)).astype(F32)
    thr = _select_threshold(lambda t: count_where(lambda kv: kv >= t), n_valid, float(k_top))
    need = float(k_top) - count_where(lambda kv: kv > thr)

    def bias_body(kt, running):
        kv = keys_sc[kt]
        eq = kv == thr
        pref = _dot(jnp.where(eq, 1.0, 0.0).astype(MXU_DTYPE), tri_ref[...])
        take_eq = jnp.where(running + pref <= need, 0.0, NEG_BIAS)
        bias_sc[kt] = jnp.where(kv > thr, 0.0, jnp.where(eq, take_eq, NEG_BIAS))
        return running + pref[:, tk - 1:tk]

    lax.fori_loop(0, nkt, bias_body, jnp.zeros((tq, 1), F32))

    for hp in range(N_HEADS_A // 2):
        g = (2 * hp) // (N_HEADS_A // N_KV_A)
        gsl = slice(g * LANES, (g + 1) * LANES)
        o_pair = []
        for sub in range(2):
            qh = qap_sc[2 * hp + sub]

            def att_body(kt, c, qh=qh, gsl=gsl):
                m, l, acc = c
                k0 = pl.multiple_of(kt * tk, tk)
                kb = kk_ref[0, pl.ds(k0, tk), gsl]
                vb = vv_ref[0, pl.ds(k0, tk), gsl]
                s = _dot_nt(qh, kb) * HEAD_DIM_A ** -0.5 + bias_sc[kt]
                m_new = jnp.maximum(m, jnp.max(s, axis=-1, keepdims=True))
                alpha = jnp.exp(m - m_new)
                p = jnp.exp(s - m_new)
                l = alpha * l + jnp.sum(p, axis=-1, keepdims=True)
                acc = alpha * acc + _dot(p.astype(MXU_DTYPE), vb)
                return m_new, l, acc

            init = (jnp.full((tq, 1), NEG_BIAS, F32), jnp.zeros((tq, 1), F32), jnp.zeros((tq, LANES), F32))
            _, l, acc = lax.fori_loop(0, nkt, att_body, init)
            o_pair.append(acc / l)
        out_ref[0, :, hp * LANES:(hp + 1) * LANES] = jnp.where(lo_half, o_pair[0], o_pair[1]).astype(out_ref.dtype)


def _dsa_call(qa, iq, iw, ikd, kk, vv, tq):
    b, s, _ = qa.shape
    nq = s // tq
    k_top = min(TOPK_MAX, s // 4)
    tri = (np.arange(tq)[:, None] <= np.arange(tq)[None, :]).astype(np.float32)
    tri = jnp.asarray(tri, MXU_DTYPE)
    qspec = lambda w_: pl.BlockSpec((1, tq, w_), lambda bb, jj: (bb, jj, 0))
    sspec = lambda w_: pl.BlockSpec((1, s, w_), lambda bb, jj: (bb, 0, 0))
    return pl.pallas_call(
        functools.partial(_dsa_kernel, tq=tq, k_top=k_top),
        grid=(b, nq),
        in_specs=[qspec(512), qspec(512), qspec(N_IDX_HEADS), sspec(128), sspec(256), sspec(256),
                  pl.BlockSpec((tq, tq), lambda bb, jj: (0, 0))],
        out_specs=qspec(512),
        out_shape=jax.ShapeDtypeStruct((b, s, 512), MXU_DTYPE),
        scratch_shapes=[pltpu.VMEM((nq, tq, tq), I32), pltpu.VMEM((nq, tq, tq), F32),
                        pltpu.VMEM((N_IDX_HEADS, tq, LANES), MXU_DTYPE),
                        pltpu.VMEM((N_HEADS_A, tq, LANES), MXU_DTYPE),
                        pltpu.VMEM((N_IDX_HEADS, tq, LANES), F32)],
        compiler_params=pltpu.CompilerParams(
            dimension_semantics=("arbitrary", "arbitrary"), vmem_limit_bytes=VMEM_LIMIT),
        name="dsa_prompt",
    )(qa, iq, iw, ikd, kk, vv, tri)


def _ret_kernel(rq_ref, rk_ref, rv_ref, rg_ref, st_ref, dmask_ref, cross_ref, kdec_ref, gdec_ref,
                gng_ref, gnb_ref, out_ref, st_out_ref, st_sc):
    c = pl.program_id(1)

    @pl.when(c == 0)
    def _():
        st_sc[:, :DK_R, :] = st_ref[0]
        st_sc[:, DK_R:, :] = jnp.zeros((N_HEADS_R, LANES - DK_R, DV_R), F32)

    for h in range(N_HEADS_R):
        sl = slice(h * LANES, (h + 1) * LANES)
        q = rq_ref[0, :, sl]
        k = rk_ref[0, :, sl]
        v = rv_ref[0, :, sl]
        state = st_sc[h]
        inner = _dot_nt(q, k) * dmask_ref[h]
        o = _dot(inner.astype(MXU_DTYPE), v) + _dot(q, state.astype(MXU_DTYPE)) * cross_ref[:, sl]
        kd = (k.astype(F32) * kdec_ref[:, sl]).astype(MXU_DTYPE)
        st_sc[h] = state * gdec_ref[h] + _dot_tn(kd, v)

        mu = jnp.mean(o, axis=-1, keepdims=True)
        d = o - mu
        var = jnp.mean(d * d, axis=-1, keepdims=True)
        y = d * lax.rsqrt(var + EPS) * gng_ref[:, sl] + gnb_ref[:, sl]
        gate = rg_ref[0, :, sl]
        out_ref[0, :, sl] = (gate * (1.0 / (1.0 + jnp.exp(-gate))) * y).astype(out_ref.dtype)

    @pl.when(c == pl.num_programs(1) - 1)
    def _():
        st_out_ref[0] = st_sc[:, :DK_R, :]


def _ret_tables(cr, c_eff):
    log_g = jnp.log(1.0 - 2.0 ** (-5.0 - jnp.arange(N_HEADS_R, dtype=F32)))
    i = jnp.arange(cr, dtype=F32)
    diff = i[:, None] - i[None, :]
    dmask = jnp.where(diff >= 0, jnp.exp(jnp.maximum(diff, 0.0)[None] * log_g[:, None, None]), 0.0)
    cross = jnp.exp((i + 1.0)[:, None] * log_g[None, :])
    kdec = jnp.where((i < c_eff)[:, None], jnp.exp((c_eff - 1.0 - i)[:, None] * log_g[None, :]), 0.0)
    gdec = jnp.exp(c_eff * log_g)
    lanes = lambda t: jnp.repeat(t, LANES, axis=1)
    return (dmask, lanes(cross), lanes(kdec),
            jnp.broadcast_to(gdec[:, None, None], (N_HEADS_R, 1, LANES)))


def _ret_call(rq, rk, rv, rg, state, gng, gnb, cr, c_eff):
    b, s, _ = rq.shape
    nc = s // cr
    dmask, cross, kdec, gdec = _ret_tables(cr, c_eff)
    tspec = pl.BlockSpec((1, cr, 512), lambda bb, cc: (bb, cc, 0))
    sspec = pl.BlockSpec((1, N_HEADS_R, DK_R, DV_R), lambda bb, cc: (bb, 0, 0, 0))
    const = lambda shape: pl.BlockSpec(shape, lambda bb, cc: (0,) * len(shape))
    return pl.pallas_call(
        _ret_kernel,
        grid=(b, nc),
        in_specs=[tspec, tspec, tspec, tspec, sspec,
                  const((N_HEADS_R, cr, cr)), const((cr, 512)), const((cr, 512)),
                  const((N_HEADS_R, 1, LANES)), const((1, 512)), const((1, 512))],
        out_specs=(tspec, sspec),
        out_shape=(jax.ShapeDtypeStruct((b, s, 512), MXU_DTYPE),
                   jax.ShapeDtypeStruct((b, N_HEADS_R, DK_R, DV_R), F32)),
        scratch_shapes=[pltpu.VMEM((N_HEADS_R, LANES, DV_R), F32)],
        compiler_params=pltpu.CompilerParams(dimension_semantics=("arbitrary", "arbitrary")),
        name="retention",
    )(rq, rk, rv, rg, state, dmask, cross, kdec, gdec, gng, gnb)


def _mlp_kernel(x_ref, a_ref, r_ref, wo_ref, wu_ref, wd_ref, g1_ref, g2_ref, g3_ref, out_ref, *, ff_chunk):
    def rms(t, g_ref):
        return t * lax.rsqrt(jnp.mean(t * t, axis=-1, keepdims=True) + EPS) * g_ref[...]

    half = a_ref.shape[1]
    mix = _dot(a_ref[...], wo_ref[:half, :]) + _dot(r_ref[...], wo_ref[half:, :])
    x1 = x_ref[...] + rms(mix, g1_ref)
    h2 = rms(x1, g2_ref).astype(MXU_DTYPE)
    acc = jnp.zeros(x1.shape, F32)
    for c in range(wu_ref.shape[1] // ff_chunk):
        sl = slice(c * ff_chunk, (c + 1) * ff_chunk)
        u = jnp.maximum(_dot(h2, wu_ref[:, sl]), 0.0)
        acc = acc + _dot((u * u).astype(MXU_DTYPE), wd_ref[sl, :])
    out_ref[...] = x1 + rms(acc, g3_ref)


def _mlp_call(x2, a, r, wo, wu, wd, g1, g2, g3, tm):
    n, d = x2.shape
    dff = wu.shape[1]
    tspec = lambda w_: pl.BlockSpec((tm, w_), lambda i: (i, 0))
    const = lambda shape: pl.BlockSpec(shape, lambda i: (0, 0))
    return pl.pallas_call(
        functools.partial(_mlp_kernel, ff_chunk=1024),
        grid=(n // tm,),
        in_specs=[tspec(d), tspec(512), tspec(512), const((d, d)), const((d, dff)), const((dff, d)),
                  const((1, d)), const((1, d)), const((1, d))],
        out_specs=tspec(d),
        out_shape=jax.ShapeDtypeStruct((n, d), F32),
        compiler_params=pltpu.CompilerParams(
            dimension_semantics=("arbitrary",), vmem_limit_bytes=VMEM_LIMIT),
        name="mix_mlp",
    )(x2, a, r, wo, wu, wd, g1, g2, g3)


PAGES_PER_STEP = 8
IDX_ROWS = 8


def _idx_sel_kernel(pt_ref, q_ref, w_ref, knew_ref, tri_ref, *rest, n_pages, k_top, t_new):
    pages = rest[:PAGES_PER_STEP]
    bias_ref = rest[PAGES_PER_STEP]
    keys_sc = rest[PAGES_PER_STEP + 1]
    i = pl.program_id(1)
    n_tiles = n_pages + 1
    q = q_ref[0]
    w = w_ref[0]

    def scores(kpage):
        s = jnp.maximum(_dot_nt(q, kpage), 0.0) * w
        acc = s[0:IDX_ROWS]
        for h in range(1, N_IDX_HEADS):
            acc = acc + s[h * IDX_ROWS:(h + 1) * IDX_ROWS]
        return _sortable_key(acc)

    for gg in range(PAGES_PER_STEP):
        keys_sc[i * PAGES_PER_STEP + gg] = scores(pages[gg][...].astype(MXU_DTYPE))

    @pl.when(i == pl.num_programs(1) - 1)
    def _():
        row = lax.broadcasted_iota(I32, (IDX_ROWS, LANES), 0)
        col = lax.broadcasted_iota(I32, (IDX_ROWS, LANES), 1)
        new_ok = jnp.logical_and(col <= row, col < t_new)
        keys_sc[n_pages] = jnp.where(new_ok, scores(knew_ref[0]), INT_MIN)

        keys = keys_sc[...]

        def count_where(pred):
            part = jnp.sum(jnp.where(pred(keys), 1.0, 0.0), axis=0)
            return jnp.sum(part, axis=-1, keepdims=True)

        t_row = lax.broadcasted_iota(I32, (IDX_ROWS, 1), 0)
        n_valid = (jnp.minimum(t_row, t_new - 1) + (n_pages * PAGE_SIZE + 1)).astype(F32)
        thr = _select_threshold(lambda t: count_where(lambda kv: kv >= t[None]), n_valid, float(k_top))
        need = float(k_top) - count_where(lambda kv: kv > thr[None])

        eq = jnp.where(keys == thr[None], 1.0, 0.0).astype(MXU_DTYPE)
        pref = _dot(eq.reshape(n_tiles * IDX_ROWS, LANES), tri_ref[...]).reshape(n_tiles, IDX_ROWS, LANES)
        running = jnp.zeros((IDX_ROWS, 1), F32)
        for t in range(n_tiles):
            kv = keys[t]
            take_eq = jnp.where(running + pref[t] <= need, 0.0, NEG_BIAS)
            bias_ref[0, :, t * LANES:(t + 1) * LANES] = jnp.where(
                kv > thr, 0.0, jnp.where(kv == thr, take_eq, NEG_BIAS))
            running = running + pref[t][:, LANES - 1:LANES]


def _idx_sel_call(page_table, q2, w2, knew, cache_kidx, k_top, t_new):
    db, n_pages = page_table.shape
    n_tiles = n_pages + 1
    tri = jnp.asarray((np.arange(LANES)[:, None] <= np.arange(LANES)[None, :]).astype(np.float32), MXU_DTYPE)
    bspec = lambda shape: pl.BlockSpec((1,) + shape, lambda b, i, pt: (b, 0, 0))
    page_specs = [pl.BlockSpec((None, PAGE_SIZE, D_IDX),
                               functools.partial(lambda b, i, pt, gg: (pt[b, i * PAGES_PER_STEP + gg], 0, 0), gg=gg))
                  for gg in range(PAGES_PER_STEP)]
    return pl.pallas_call(
        functools.partial(_idx_sel_kernel, n_pages=n_pages, k_top=k_top, t_new=t_new),
        grid_spec=pltpu.PrefetchScalarGridSpec(
            num_scalar_prefetch=1,
            grid=(db, n_pages // PAGES_PER_STEP),
            in_specs=[bspec((N_IDX_HEADS * IDX_ROWS, D_IDX)), bspec((N_IDX_HEADS * IDX_ROWS, LANES)),
                      bspec((PAGE_SIZE, D_IDX)),
                      pl.BlockSpec((LANES, LANES), lambda b, i, pt: (0, 0))] + page_specs,
            out_specs=bspec((IDX_ROWS, n_tiles * LANES)),
            scratch_shapes=[pltpu.VMEM((n_tiles, IDX_ROWS, LANES), I32)]),
        out_shape=jax.ShapeDtypeStruct((db, IDX_ROWS, n_tiles * LANES), F32),
        compiler_params=pltpu.CompilerParams(dimension_semantics=("arbitrary", "arbitrary")),
        name="dsa_sample_select",
    )(page_table, q2, w2, knew, tri, *([cache_kidx] * PAGES_PER_STEP))


def _paged_attn_kernel(pt_ref, q_ref, bias_ref, bias_new_ref, knew_ref, vnew_ref, *rest, t_new):
    kpages = rest[:PAGES_PER_STEP]
    vpages = rest[PAGES_PER_STEP:2 * PAGES_PER_STEP]
    out_ref, m_sc, l_sc, acc_sc = rest[2 * PAGES_PER_STEP:]
    i = pl.program_id(1)
    rows = t_new * N_HEADS_A

    @pl.when(i == 0)
    def _():
        m_sc[...] = jnp.full(m_sc.shape, NEG_BIAS, F32)
        l_sc[...] = jnp.zeros(l_sc.shape, F32)
        acc_sc[...] = jnp.zeros(acc_sc.shape, F32)

    q = q_ref[0]

    def expand(b8):
        return jnp.concatenate(
            [jnp.broadcast_to(b8[t:t + 1, :], (N_HEADS_A, LANES)) for t in range(t_new)], axis=0)

    def step(kb, vb, bias):
        s = _dot_nt(q, kb) * HEAD_DIM_A ** -0.5 + expand(bias)
        m = m_sc[...]
        m_new = jnp.maximum(m, jnp.max(s, axis=-1, keepdims=True))
        alpha = jnp.exp(m - m_new)
        p = jnp.exp(s - m_new)
        l_sc[...] = alpha * l_sc[...] + jnp.sum(p, axis=-1, keepdims=True)
        acc_sc[...] = alpha * acc_sc[...] + _dot(p.astype(MXU_DTYPE), vb)
        m_sc[...] = m_new

    for gg in range(PAGES_PER_STEP):
        step(kpages[gg][...].astype(MXU_DTYPE), vpages[gg][...].astype(MXU_DTYPE),
             bias_ref[0, :, gg * LANES:(gg + 1) * LANES])

    @pl.when(i == pl.num_programs(1) - 1)
    def _():
        step(knew_ref[0], vnew_ref[0], bias_new_ref[0])
        out_ref[0] = acc_sc[...] / l_sc[...]


def _paged_attn_call(page_table, qp, bias, knew, vnew, cache_k, cache_v, t_new):
    db, n_pages = page_table.shape
    rows = t_new * N_HEADS_A
    kv_w = N_KV_A * HEAD_DIM_A
    bspec = lambda shape: pl.BlockSpec((1,) + shape, lambda b, i, pt: (b, 0, 0))
    pspec = lambda gg: pl.BlockSpec(
        (None, PAGE_SIZE, kv_w),
        functools.partial(lambda b, i, pt, gg: (pt[b, i * PAGES_PER_STEP + gg], 0, 0), gg=gg))
    return pl.pallas_call(
        functools.partial(_paged_attn_kernel, t_new=t_new),
        grid_spec=pltpu.PrefetchScalarGridSpec(
            num_scalar_prefetch=1,
            grid=(db, n_pages // PAGES_PER_STEP),
            in_specs=[bspec((rows, kv_w)),
                      pl.BlockSpec((1, IDX_ROWS, PAGES_PER_STEP * LANES), lambda b, i, pt: (b, 0, i)),
                      pl.BlockSpec((1, IDX_ROWS, LANES), lambda b, i, pt: (b, 0, n_pages)),
                      bspec((PAGE_SIZE, kv_w)), bspec((PAGE_SIZE, kv_w))]
                     + [pspec(gg) for gg in range(PAGES_PER_STEP)] * 2,
            out_specs=bspec((rows, kv_w)),
            scratch_shapes=[pltpu.VMEM((rows, 1), F32), pltpu.VMEM((rows, 1), F32),
                            pltpu.VMEM((rows, kv_w), F32)]),
        out_shape=jax.ShapeDtypeStruct((db, rows, kv_w), F32),
        compiler_params=pltpu.CompilerParams(dimension_semantics=("arbitrary", "arbitrary")),
        name="dsa_sample_attend",
    )(page_table, qp, bias, bias, knew, vnew, *([cache_k] * PAGES_PER_STEP), *([cache_v] * PAGES_PER_STEP))


def _layout_w_in(w_in):
    d = w_in.shape[0]
    cuts = np.cumsum(PROJ_SIZES)[:-1].tolist()
    wq, wk, wv, wiq, wik, wiw, wrq, wrk, wrv, wrg = jnp.split(w_in, cuts, axis=1)
    dup = lambda w_: jnp.concatenate(
        [w_[:, g * HEAD_DIM_A:(g + 1) * HEAD_DIM_A] for g in range(N_KV_A) for _ in range(2)], axis=1)
    pad_heads = lambda w_: jnp.pad(
        w_.reshape(d, N_HEADS_R, DK_R), ((0, 0), (0, 0), (0, LANES - DK_R))).reshape(d, N_HEADS_R * LANES)
    ikw = jnp.pad(jnp.concatenate([wik, wiw], axis=1), ((0, 0), (0, LANES - D_IDX - N_IDX_HEADS)))
    w = jnp.concatenate([wq, wk, wv, dup(wk), dup(wv), wiq, ikw, pad_heads(wrq), pad_heads(wrk), wrv, wrg], axis=1)
    assert w.shape[1] == W_PROJ
    return w.astype(MXU_DTYPE)


def _rope_tables(pos):
    posf = pos.astype(F32)[:, None]

    def tables(rot_dim, theta, period, width):
        half = rot_dim // 2
        inv = theta ** (-jnp.arange(half, dtype=F32) / half)
        ang = posf * inv[None, :]
        cos, sin = jnp.cos(ang), jnp.sin(ang)
        n = pos.shape[0]
        rest = period - rot_dim
        c = jnp.concatenate([cos, cos, jnp.ones((n, rest), F32)], axis=1)
        su = jnp.concatenate([-sin, jnp.zeros((n, half + rest), F32)], axis=1)
        sd = jnp.concatenate([jnp.zeros((n, half), F32), sin, jnp.zeros((n, rest), F32)], axis=1)
        return [jnp.tile(t, (1, width // period)) for t in (c, su, sd)]

    a = tables(ROT_DIM_A, ROPE_THETA, HEAD_DIM_A, LANES)
    r = tables(DK_R, RET_THETA, LANES, LANES)
    return jnp.concatenate(a + r, axis=1)


def kernel(x_prompt, x_sample, cache_k, cache_v, cache_kidx, state_ret, page_table, w_in, w_out, w_up, w_down,
           g_pre_mix, g_post_mix, g_pre_mlp, g_post_mlp, kidx_ln_g, kidx_ln_b, ret_gn_g, ret_gn_b):
    b, s, d = x_prompt.shape
    db, t_new, _ = x_sample.shape
    n_pages = page_table.shape[1]
    past = n_pages * PAGE_SIZE
    layer = 0
    assert w_in.shape[0] == 1 and t_new <= IDX_ROWS and n_pages % PAGES_PER_STEP == 0

    w_proj = _layout_w_in(w_in[layer])
    wo, wu, wd = (w_[layer].astype(MXU_DTYPE) for w_ in (w_out, w_up, w_down))
    row = lambda v: v[layer][None, :].astype(F32)
    g_pre, g1, g2, g3 = row(g_pre_mix), row(g_post_mix), row(g_pre_mlp), row(g_post_mlp)
    lng = jnp.pad(row(kidx_ln_g), ((0, 0), (0, LANES - D_IDX)))
    lnb = jnp.pad(row(kidx_ln_b), ((0, 0), (0, LANES - D_IDX)))
    gng, gnb = row(ret_gn_g), row(ret_gn_b)

    tm = min(512, s)
    tq = min(256, s)
    pp = _proj_call(x_prompt, _rope_tables(jnp.arange(s)), g_pre, w_proj, lng, lnb, tm)
    a_out = _dsa_call(pp["qa"], pp["iq"], pp["iw"], pp["ikd"], pp["kk"], pp["vv"], tq)
    zero_state = jnp.zeros((b, N_HEADS_R, DK_R, DV_R), F32)
    r_out, ret_prompt = _ret_call(pp["rq"], pp["rk"], pp["rv"], pp["rg"], zero_state, gng, gnb,
                                  RET_CHUNK, float(RET_CHUNK))
    y_prompt = _mlp_call(x_prompt.reshape(b * s, d), a_out.reshape(b * s, 512), r_out.reshape(b * s, 512),
                         wo, wu, wd, g1, g2, g3, tm).reshape(b, s, d)

    sr = SAMPLE_ROWS
    xs = jnp.pad(x_sample, ((0, 0), (0, sr - t_new), (0, 0))).reshape(1, db * sr, d)
    pos_s = jnp.tile(past + jnp.arange(sr), db)
    n_s = db * sr
    tm_s = min(512, n_s)
    ps = _proj_call(xs, _rope_tables(pos_s), g_pre, w_proj, lng, lnb, tm_s)
    per_seq = lambda t: t.reshape(db, sr, t.shape[-1])
    ps = {k_: per_seq(v_) for k_, v_ in ps.items()}

    k_top = min(TOPK_MAX, (past + t_new) // 4)
    iq_s = ps["iq"][:, :IDX_ROWS].reshape(db, IDX_ROWS, N_IDX_HEADS, D_IDX)
    q2 = iq_s.transpose(0, 2, 1, 3).reshape(db, N_IDX_HEADS * IDX_ROWS, D_IDX)
    w2 = jnp.broadcast_to(ps["iw"][:, :IDX_ROWS].transpose(0, 2, 1).reshape(db, N_IDX_HEADS * IDX_ROWS, 1),
                          (db, N_IDX_HEADS * IDX_ROWS, LANES))
    pad_page = lambda t: jnp.pad(t, ((0, 0), (0, PAGE_SIZE - sr), (0, 0)))
    knew_idx = pad_page(ps["ikd"][:, :, :D_IDX])
    bias = _idx_sel_call(page_table, q2, w2, knew_idx, cache_kidx[layer], k_top, t_new)

    qa_s = ps["qa"][:, :t_new].reshape(db, t_new, N_HEADS_A, HEAD_DIM_A)
    grp = (jnp.arange(N_HEADS_A) // (N_HEADS_A // N_KV_A))[None, None, :, None, None]
    qp = jnp.where(grp == jnp.arange(N_KV_A)[None, None, None, :, None], qa_s[:, :, :, None, :],
                   jnp.zeros((), MXU_DTYPE)).reshape(db, t_new * N_HEADS_A, N_KV_A * HEAD_DIM_A)
    kv_w = N_KV_A * HEAD_DIM_A
    knew = pad_page(ps["ka"].astype(MXU_DTYPE))
    vnew = pad_page(ps["va"].astype(MXU_DTYPE))
    o_s = _paged_attn_call(page_table, qp, bias, knew, vnew,
                           cache_k[layer].reshape(-1, PAGE_SIZE, kv_w), cache_v[layer].reshape(-1, PAGE_SIZE, kv_w),
                           t_new)
    o_s = o_s.reshape(db, t_new, N_HEADS_A, N_KV_A, HEAD_DIM_A)
    a_s = jnp.where(grp == jnp.arange(N_KV_A)[None, None, None, :, None], o_s, 0.0).sum(axis=3)
    a_s = jnp.pad(a_s.reshape(db, t_new, 512), ((0, 0), (0, sr - t_new), (0, 0))).astype(MXU_DTYPE)

    r_s, ret_sample = _ret_call(ps["rq"], ps["rk"], ps["rv"], ps["rg"], state_ret[layer].astype(F32), gng, gnb,
                                sr, float(t_new))
    y_s = _mlp_call(xs.reshape(n_s, d), a_s.reshape(n_s, 512), r_s.reshape(n_s, 512),
                    wo, wu, wd, g1, g2, g3, tm_s).reshape(db, sr, d)[:, :t_new]

    kv_shape = lambda n_b, n_t: (1, n_b, n_t, N_KV_A, HEAD_DIM_A)
    return (y_prompt, y_s,
            pp["ka"].reshape(kv_shape(b, s)), pp["va"].reshape(kv_shape(b, s)), pp["ik"][None],
            ret_prompt[None],
            ps["ka"][:, :t_new].reshape(kv_shape(db, t_new)), ps["va"][:, :t_new].reshape(kv_shape(db, t_new)),
            ps["ik"][None, :, :t_new], ret_sample[None])
```

```python
import functools

import numpy as np
import jax
import jax.numpy as jnp
from jax import lax
from jax.experimental import pallas as pl
from jax.experimental.pallas import tpu as pltpu

F32 = jnp.float32
I32 = jnp.int32
MXU_DTYPE = jnp.bfloat16

LANES = 128
SUBLANES = 8
N_HEADS_A = 8
N_KV_A = 2
HEAD_DIM_A = 64
ROT_DIM_A = 16
ROPE_THETA = 500000.0
N_IDX_HEADS = 8
D_IDX = 64
TOPK_MAX = 256
N_HEADS_R = 4
DV_R = 128
DK_R = 64
RET_THETA = 10000.0
RET_CHUNK = 128
PAGE_SIZE = 128
EPS = 1e-6
PROJ_SIZES = (N_HEADS_A * HEAD_DIM_A, N_KV_A * HEAD_DIM_A, N_KV_A * HEAD_DIM_A,
              N_IDX_HEADS * D_IDX, D_IDX, N_IDX_HEADS,
              N_HEADS_R * DK_R, N_HEADS_R * DK_R, N_HEADS_R * DV_R, N_HEADS_R * DV_R)
KV_W = N_KV_A * HEAD_DIM_A

INT_MIN = -(2 ** 31)
NEG_BIAS = -1e30
SAMPLE_ROWS = 16
KEY_TILE = 256
VMEM_LIMIT = 56 * 1024 * 1024

_SEG = {}
_off = 0
for _name, _w in (("q", 512), ("k", 128), ("v", 128), ("kk", 256), ("iq", 512),
                  ("ikw", 128), ("rq", 512), ("rk", 512), ("rv", 512), ("rg", 512)):
    _SEG[_name] = (_off, _off + _w)
    _off += _w
W_PROJ = _off


def _dot(a, b):
    return jnp.dot(a, b, preferred_element_type=F32)


def _dot_nt(a, b):
    return lax.dot_general(a, b, (((1,), (1,)), ((), ())), preferred_element_type=F32)


def _dot_tn(a, b):
    return lax.dot_general(a, b, (((0,), (0,)), ((), ())), preferred_element_type=F32)


def _tile_lanes(t, width):
    reps = width // t.shape[1]
    return t if reps == 1 else jnp.concatenate([t] * reps, axis=1)


def _sortable_key(score):
    score = jnp.where(score == 0.0, 0.0, score)
    bits = pltpu.bitcast(score, I32)
    return jnp.where(bits >= 0, bits, bits ^ 0x7FFFFFFF)


def _select_threshold(count_ge, n_valid, k_top):
    def cond(c):
        bit, _, cnt = c
        return jnp.logical_and(bit >= 0, jnp.max(cnt) > k_top)

    def body(c):
        bit, u, cnt = c
        cand = u | lax.shift_left(jnp.int32(1), bit)
        c_new = count_ge(cand ^ INT_MIN)
        ok = c_new >= k_top
        return bit - 1, jnp.where(ok, cand, u), jnp.where(ok, c_new, cnt)

    u0 = jnp.zeros(n_valid.shape, I32)
    _, u, _ = lax.while_loop(cond, body, (jnp.int32(31), u0, n_valid))
    return jnp.maximum(u ^ INT_MIN, INT_MIN + 1)


def _rope(z, cos_t, sin_up_t, sin_dn_t, half):
    w = z.shape[1]
    up = pltpu.roll(z, w - half, 1)
    dn = pltpu.roll(z, half, 1)
    return z * _tile_lanes(cos_t, w) + up * _tile_lanes(sin_up_t, w) + dn * _tile_lanes(sin_dn_t, w)


def _proj_kernel(x_ref, tab_ref, g_ref, w_ref, lng_ref, lnb_ref,
                 qa_o, kk_o, iq_o, ikd_o, rq_o, rk_o, rv_o, rg_o, kt_o, vt_o, vtb_o, ikt_o, iwt_o):
    x = x_ref[0]
    h = (x * lax.rsqrt(jnp.mean(x * x, axis=-1, keepdims=True) + EPS) * g_ref[...]).astype(MXU_DTYPE)

    def seg(name):
        lo, hi = _SEG[name]
        return _dot(h, w_ref[:, lo:hi])

    tab = tab_ref[...]
    c_a, su_a, sd_a, c_r, su_r, sd_r = [tab[:, i * LANES:(i + 1) * LANES] for i in range(6)]
    half_a = ROT_DIM_A // 2
    half_r = DK_R // 2

    qa_o[0] = _rope(seg("q"), c_a, su_a, sd_a, half_a).astype(qa_o.dtype)
    kk_o[0] = _rope(seg("kk"), c_a, su_a, sd_a, half_a).astype(kk_o.dtype)
    iq_o[0] = _rope(seg("iq"), c_a, su_a, sd_a, half_a).astype(iq_o.dtype)
    rq_o[0] = _rope(seg("rq"), c_r, su_r, sd_r, half_r).astype(rq_o.dtype)
    rk_o[0] = _rope(seg("rk") * DK_R ** -0.5, c_r, su_r, sd_r, half_r).astype(rk_o.dtype)
    rv_o[0] = seg("rv").astype(rv_o.dtype)
    rg_o[0] = seg("rg")

    kt_o[0] = _rope(seg("k"), c_a, su_a, sd_a, half_a).T
    vt = seg("v").T
    vt_o[0] = vt
    for t in range(vtb_o.shape[1]):
        vtb_o[0, t] = vt[:, t * KEY_TILE:(t + 1) * KEY_TILE].astype(vtb_o.dtype)

    zi = seg("ikw")
    lane = lax.broadcasted_iota(I32, zi.shape, 1)
    is_ik = lane < D_IDX
    mu = jnp.sum(jnp.where(is_ik, zi, 0.0), axis=-1, keepdims=True) * (1.0 / D_IDX)
    d = jnp.where(is_ik, zi - mu, 0.0)
    var = jnp.sum(d * d, axis=-1, keepdims=True) * (1.0 / D_IDX)
    y = d * lax.rsqrt(var + EPS) * lng_ref[...] + lnb_ref[...]
    ikr = jnp.where(is_ik, _rope(y, c_a, su_a, sd_a, half_a), 0.0)
    ikd_o[0] = (ikr + pltpu.roll(ikr, D_IDX, 1)).astype(ikd_o.dtype)
    both_t = jnp.where(is_ik, ikr, zi * (N_IDX_HEADS * D_IDX) ** -0.5).T
    ikt_o[0] = both_t[:D_IDX]
    iwt_o[0] = both_t[D_IDX:D_IDX + N_IDX_HEADS]


def _proj_call(x3, tab, g, w, lng, lnb, tm):
    bx, n, d = x3.shape
    grid = (n // tm, bx)
    tspec = lambda w_: pl.BlockSpec((1, tm, w_), lambda i, b: (b, i, 0))
    ttspec = lambda r_: pl.BlockSpec((1, r_, tm), lambda i, b: (b, 0, i))
    const = lambda shape: pl.BlockSpec(shape, lambda i, b: (0,) * len(shape))
    nat = (("qa", 512, MXU_DTYPE), ("kk", 256, MXU_DTYPE), ("iq", 512, MXU_DTYPE), ("ikd", 128, MXU_DTYPE),
           ("rq", 512, MXU_DTYPE), ("rk", 512, MXU_DTYPE), ("rv", 512, MXU_DTYPE), ("rg", 512, F32))
    tr = (("kT", KV_W, F32), ("vT", KV_W, F32))
    tr2 = (("ikT", D_IDX, F32), ("iwT", N_IDX_HEADS, F32))
    out_specs = ([tspec(w_) for _, w_, _ in nat] + [ttspec(r_) for _, r_, _ in tr]
                 + [pl.BlockSpec((1, tm // KEY_TILE, KV_W, KEY_TILE), lambda i, b: (b, i, 0, 0))]
                 + [ttspec(r_) for _, r_, _ in tr2])
    out_shape = ([jax.ShapeDtypeStruct((bx, n, w_), dt) for _, w_, dt in nat]
                 + [jax.ShapeDtypeStruct((bx, r_, n), dt) for _, r_, dt in tr]
                 + [jax.ShapeDtypeStruct((bx, n // KEY_TILE, KV_W, KEY_TILE), MXU_DTYPE)]
                 + [jax.ShapeDtypeStruct((bx, r_, n), dt) for _, r_, dt in tr2])
    names = [o[0] for o in nat] + [o[0] for o in tr] + ["vTb"] + [o[0] for o in tr2]
    res = pl.pallas_call(
        _proj_kernel,
        grid=grid,
        in_specs=[tspec(d),
                  pl.BlockSpec((tm, 6 * LANES), lambda i, b: (i, 0)),
                  const((1, d)), const((d, W_PROJ)), const((1, LANES)), const((1, LANES))],
        out_specs=tuple(out_specs),
        out_shape=tuple(out_shape),
        compiler_params=pltpu.CompilerParams(
            dimension_semantics=("arbitrary", "arbitrary"), vmem_limit_bytes=VMEM_LIMIT),
        name="proj",
    )(x3, tab, g, w, lng, lnb)
    return dict(zip(names, res))


def _dsa_kernel(qa_ref, iq_ref, iwt_ref, ikd_ref, kk_ref, vtb_ref, tri_ref, out_ref,
                keys_sc, bias_sc, iqp_sc, qap_sc, acc_sc, *, k_top):
    tq = tk = KEY_TILE
    j = pl.program_id(1)
    q0 = j * tq
    nkt = j + 1
    heads_per_kv = N_HEADS_A // N_KV_A

    lane = lax.broadcasted_iota(I32, (tq, LANES), 1)
    lo_half = lane < HEAD_DIM_A
    zero = jnp.zeros((), MXU_DTYPE)
    for h in range(N_HEADS_A):
        keep = lo_half if h % 2 == 0 else jnp.logical_not(lo_half)
        sl = slice((h // 2) * LANES, (h // 2 + 1) * LANES)
        iqp_sc[h] = jnp.where(keep, iq_ref[0, :, sl], zero)
        qap_sc[h] = jnp.where(keep, qa_ref[0, :, sl] * HEAD_DIM_A ** -0.5, zero)

    kpos = lax.broadcasted_iota(I32, (tk, tq), 0)
    qpos = lax.broadcasted_iota(I32, (tk, tq), 1) + q0

    def score_body(kt, carry):
        k0 = pl.multiple_of(kt * tk, tk)
        kb = ikd_ref[0, pl.ds(k0, tk), :]
        acc = jnp.zeros((tk, tq), F32)
        for h in range(N_IDX_HEADS):
            acc = acc + jnp.maximum(_dot_nt(kb, iqp_sc[h]), 0.0) * iwt_ref[0, h:h + 1, :]
        keys_sc[kt] = jnp.where(kpos + k0 <= qpos, _sortable_key(acc), INT_MIN)
        return carry

    lax.fori_loop(0, nkt, score_body, 0)

    def count_where(pred):
        def body(kt, part):
            hit = jnp.where(pred(keys_sc[kt]), 1.0, 0.0)
            return part + jnp.sum(hit.reshape(tk // SUBLANES, SUBLANES, tq), axis=0)
        part = lax.fori_loop(0, nkt, body, jnp.zeros((SUBLANES, tq), F32))
        return jnp.sum(part, axis=0, keepdims=True)

    n_valid = (lax.broadcasted_iota(I32, (1, tq), 1) + (q0 + 1)).astype(F32)
    thr = _select_threshold(lambda t: count_where(lambda kv: kv >= t), n_valid, float(k_top))
    need = float(k_top) - count_where(lambda kv: kv > thr)

    def bias_body(kt, running):
        kv = keys_sc[kt]
        eq = kv == thr
        pref = _dot(tri_ref[...], jnp.where(eq, 1.0, 0.0).astype(MXU_DTYPE))
        take_eq = jnp.where(running + pref <= need, 0.0, NEG_BIAS)
        bias_sc[kt] = jnp.where(kv > thr, 0.0, jnp.where(eq, take_eq, NEG_BIAS))
        return running + pref[tk - 1:tk, :]

    lax.fori_loop(0, nkt, bias_body, jnp.zeros((1, tq), F32))

    wide = heads_per_kv * tq
    acc_sc[...] = jnp.zeros(acc_sc.shape, F32)

    def att_body(kt, carry):
        k0 = pl.multiple_of(kt * tk, tk)
        bias = _tile_lanes(bias_sc[kt], wide)
        out = []
        for g, (m, l) in enumerate(carry):
            qg = qap_sc[g * heads_per_kv:(g + 1) * heads_per_kv].reshape(wide, LANES)
            kb = kk_ref[0, pl.ds(k0, tk), g * LANES:(g + 1) * LANES]
            vb = vtb_ref[0, kt, g * HEAD_DIM_A:(g + 1) * HEAD_DIM_A, :]
            s = _dot_nt(kb, qg) + bias
            m_new = jnp.maximum(m, jnp.max(s, axis=0, keepdims=True))
            alpha = jnp.exp(m - m_new)
            p = jnp.exp(s - m_new)
            acc_sc[g] = alpha * acc_sc[g] + _dot(vb, p.astype(MXU_DTYPE))
            psum = jnp.sum(p.reshape(tk // SUBLANES, SUBLANES, wide), axis=0)
            out.append((m_new, alpha * l + psum))
        return tuple(out)

    init = tuple((jnp.full((1, wide), NEG_BIAS, F32), jnp.zeros((SUBLANES, wide), F32)) for _ in range(N_KV_A))
    fin = lax.fori_loop(0, nkt, att_body, init)
    for g, (_, l) in enumerate(fin):
        acc_sc[g] = acc_sc[g] / jnp.sum(l, axis=0, keepdims=True)

    for hp in range(N_HEADS_A // 2):
        g, n = divmod(2 * hp, heads_per_kv)
        pair = jnp.concatenate([acc_sc[g, :, n * tq:(n + 1) * tq],
                                acc_sc[g, :, (n + 1) * tq:(n + 2) * tq]], axis=0)
        out_ref[0, :, hp * LANES:(hp + 1) * LANES] = pair.T.astype(out_ref.dtype)


def _dsa_call(qa, iq, iwt, ikd, kk, vtb):
    b, s, _ = qa.shape
    tq = KEY_TILE
    nq = s // tq
    k_top = min(TOPK_MAX, s // 4)
    tri = jnp.asarray((np.arange(tq)[:, None] >= np.arange(tq)[None, :]).astype(np.float32), MXU_DTYPE)
    qspec = lambda w_: pl.BlockSpec((1, tq, w_), lambda bb, jj: (bb, jj, 0))
    sspec = lambda w_: pl.BlockSpec((1, s, w_), lambda bb, jj: (bb, 0, 0))
    return pl.pallas_call(
        functools.partial(_dsa_kernel, k_top=k_top),
        grid=(b, nq),
        in_specs=[qspec(512), qspec(512),
                  pl.BlockSpec((1, N_IDX_HEADS, tq), lambda bb, jj: (bb, 0, jj)),
                  sspec(128), sspec(256),
                  pl.BlockSpec((1, nq, KV_W, tq), lambda bb, jj: (bb, 0, 0, 0)),
                  pl.BlockSpec((tq, tq), lambda bb, jj: (0, 0))],
        out_specs=qspec(512),
        out_shape=jax.ShapeDtypeStruct((b, s, 512), MXU_DTYPE),
        scratch_shapes=[pltpu.VMEM((nq, tq, tq), I32), pltpu.VMEM((nq, tq, tq), F32),
                        pltpu.VMEM((N_IDX_HEADS, tq, LANES), MXU_DTYPE),
                        pltpu.VMEM((N_HEADS_A, tq, LANES), MXU_DTYPE),
                        pltpu.VMEM((N_KV_A, HEAD_DIM_A, (N_HEADS_A // N_KV_A) * tq), F32)],
        compiler_params=pltpu.CompilerParams(
            dimension_semantics=("arbitrary", "arbitrary"), vmem_limit_bytes=VMEM_LIMIT),
        name="dsa_prompt",
    )(qa, iq, iwt, ikd, kk, vtb, tri)


def _ret_kernel(rq_ref, rk_ref, rv_ref, rg_ref, st_ref, dmask_ref, cross_ref, kdec_ref, gdec_ref,
                gng_ref, gnb_ref, out_ref, st_out_ref, st_sc):
    c = pl.program_id(1)

    @pl.when(c == 0)
    def _():
        st_sc[:, :DK_R, :] = st_ref[0]
        st_sc[:, DK_R:, :] = jnp.zeros((N_HEADS_R, LANES - DK_R, DV_R), F32)

    for h in range(N_HEADS_R):
        sl = slice(h * LANES, (h + 1) * LANES)
        q = rq_ref[0, :, sl]
        k = rk_ref[0, :, sl]
        v = rv_ref[0, :, sl]
        state = st_sc[h]
        inner = _dot_nt(q, k) * dmask_ref[h]
        o = _dot(inner.astype(MXU_DTYPE), v) + _dot(q, state.astype(MXU_DTYPE)) * cross_ref[:, sl]
        kd = (k.astype(F32) * kdec_ref[:, sl]).astype(MXU_DTYPE)
        st_sc[h] = state * gdec_ref[h] + _dot_tn(kd, v)

        mu = jnp.mean(o, axis=-1, keepdims=True)
        d = o - mu
        var = jnp.mean(d * d, axis=-1, keepdims=True)
        y = d * lax.rsqrt(var + EPS) * gng_ref[:, sl] + gnb_ref[:, sl]
        gate = rg_ref[0, :, sl]
        out_ref[0, :, sl] = (gate * (1.0 / (1.0 + jnp.exp(-gate))) * y).astype(out_ref.dtype)

    @pl.when(c == pl.num_programs(1) - 1)
    def _():
        st_out_ref[0] = st_sc[:, :DK_R, :]


def _ret_tables(cr, c_eff):
    log_g = jnp.log(1.0 - 2.0 ** (-5.0 - jnp.arange(N_HEADS_R, dtype=F32)))
    i = jnp.arange(cr, dtype=F32)
    diff = i[:, None] - i[None, :]
    dmask = jnp.where(diff >= 0, jnp.exp(jnp.maximum(diff, 0.0)[None] * log_g[:, None, None]), 0.0)
    cross = jnp.exp((i + 1.0)[:, None] * log_g[None, :])
    kdec = jnp.where((i < c_eff)[:, None], jnp.exp((c_eff - 1.0 - i)[:, None] * log_g[None, :]), 0.0)
    gdec = jnp.exp(c_eff * log_g)
    lanes = lambda t: jnp.repeat(t, LANES, axis=1)
    return (dmask, lanes(cross), lanes(kdec),
            jnp.broadcast_to(gdec[:, None, None], (N_HEADS_R, 1, LANES)))


def _ret_call(rq, rk, rv, rg, state, gng, gnb, cr, c_eff):
    b, s, _ = rq.shape
    nc = s // cr
    dmask, cross, kdec, gdec = _ret_tables(cr, c_eff)
    tspec = pl.BlockSpec((1, cr, 512), lambda bb, cc: (bb, cc, 0))
    sspec = pl.BlockSpec((1, N_HEADS_R, DK_R, DV_R), lambda bb, cc: (bb, 0, 0, 0))
    const = lambda shape: pl.BlockSpec(shape, lambda bb, cc: (0,) * len(shape))
    return pl.pallas_call(
        _ret_kernel,
        grid=(b, nc),
        in_specs=[tspec, tspec, tspec, tspec, sspec,
                  const((N_HEADS_R, cr, cr)), const((cr, 512)), const((cr, 512)),
                  const((N_HEADS_R, 1, LANES)), const((1, 512)), const((1, 512))],
        out_specs=(tspec, sspec),
        out_shape=(jax.ShapeDtypeStruct((b, s, 512), MXU_DTYPE),
                   jax.ShapeDtypeStruct((b, N_HEADS_R, DK_R, DV_R), F32)),
        scratch_shapes=[pltpu.VMEM((N_HEADS_R, LANES, DV_R), F32)],
        compiler_params=pltpu.CompilerParams(dimension_semantics=("arbitrary", "arbitrary")),
        name="retention",
    )(rq, rk, rv, rg, state, dmask, cross, kdec, gdec, gng, gnb)


def _mlp_kernel(x_ref, a_ref, r_ref, wo_ref, wu_ref, wd_ref, g1_ref, g2_ref, g3_ref, out_ref, *, ff_chunk):
    def rms(t, g_ref):
        return t * lax.rsqrt(jnp.mean(t * t, axis=-1, keepdims=True) + EPS) * g_ref[...]

    half = a_ref.shape[1]
    mix = _dot(a_ref[...], wo_ref[:half, :]) + _dot(r_ref[...], wo_ref[half:, :])
    x1 = x_ref[...] + rms(mix, g1_ref)
    h2 = rms(x1, g2_ref).astype(MXU_DTYPE)
    acc = jnp.zeros(x1.shape, F32)
    for c in range(wu_ref.shape[1] // ff_chunk):
        sl = slice(c * ff_chunk, (c + 1) * ff_chunk)
        u = jnp.maximum(_dot(h2, wu_ref[:, sl]), 0.0)
        acc = acc + _dot((u * u).astype(MXU_DTYPE), wd_ref[sl, :])
    out_ref[...] = x1 + rms(acc, g3_ref)


def _mlp_call(x2, a, r, wo, wu, wd, g1, g2, g3, tm):
    n, d = x2.shape
    dff = wu.shape[1]
    tspec = lambda w_: pl.BlockSpec((tm, w_), lambda i: (i, 0))
    const = lambda shape: pl.BlockSpec(shape, lambda i: (0, 0))
    return pl.pallas_call(
        functools.partial(_mlp_kernel, ff_chunk=1024),
        grid=(n // tm,),
        in_specs=[tspec(d), tspec(512), tspec(512), const((d, d)), const((d, dff)), const((dff, d)),
                  const((1, d)), const((1, d)), const((1, d))],
        out_specs=tspec(d),
        out_shape=jax.ShapeDtypeStruct((n, d), F32),
        compiler_params=pltpu.CompilerParams(
            dimension_semantics=("arbitrary",), vmem_limit_bytes=VMEM_LIMIT),
        name="mix_mlp",
    )(x2, a, r, wo, wu, wd, g1, g2, g3)


PAGES_PER_STEP = 16
IDX_ROWS = 8
SELECT_SEQS = 16


def _page_specs(n_feat):
    return [pl.BlockSpec((None, n_feat, PAGE_SIZE),
                         functools.partial(lambda b, i, pt, gg: (pt[b, i * PAGES_PER_STEP + gg], 0, 0), gg=gg))
            for gg in range(PAGES_PER_STEP)]


def _idx_scores_kernel(pt_ref, q_ref, w_ref, knew_ref, *rest, n_pages, t_new):
    pages = rest[:PAGES_PER_STEP]
    keys_ref = rest[PAGES_PER_STEP]
    i = pl.program_id(1)
    q = q_ref[0]
    w = w_ref[0]

    def scores(kpage_t):
        s = jnp.maximum(_dot(q, kpage_t), 0.0) * w
        acc = s[0:IDX_ROWS]
        for h in range(1, N_IDX_HEADS):
            acc = acc + s[h * IDX_ROWS:(h + 1) * IDX_ROWS]
        return _sortable_key(acc)

    for gg in range(PAGES_PER_STEP):
        keys_ref[0, i * PAGES_PER_STEP + gg] = scores(pages[gg][...].astype(MXU_DTYPE))

    @pl.when(i == pl.num_programs(1) - 1)
    def _():
        row = lax.broadcasted_iota(I32, (IDX_ROWS, LANES), 0)
        col = lax.broadcasted_iota(I32, (IDX_ROWS, LANES), 1)
        new_ok = jnp.logical_and(col <= row, col < t_new)
        keys_ref[0, n_pages] = jnp.where(new_ok, scores(knew_ref[0]), INT_MIN)


def _idx_scores_call(page_table, q2, w2, knew_t, kidx_t, t_new):
    db, n_pages = page_table.shape
    n_tiles = n_pages + 1
    bspec = lambda shape: pl.BlockSpec((1,) + shape, lambda b, i, pt: (b,) + (0,) * len(shape))
    return pl.pallas_call(
        functools.partial(_idx_scores_kernel, n_pages=n_pages, t_new=t_new),
        grid_spec=pltpu.PrefetchScalarGridSpec(
            num_scalar_prefetch=1,
            grid=(db, n_pages // PAGES_PER_STEP),
            in_specs=[bspec((N_IDX_HEADS * IDX_ROWS, D_IDX)), bspec((N_IDX_HEADS * IDX_ROWS, LANES)),
                      bspec((D_IDX, PAGE_SIZE))] + _page_specs(D_IDX),
            out_specs=bspec((n_tiles, IDX_ROWS, LANES))),
        out_shape=jax.ShapeDtypeStruct((db, n_tiles, IDX_ROWS, LANES), I32),
        compiler_params=pltpu.CompilerParams(dimension_semantics=("arbitrary", "arbitrary")),
        name="dsa_sample_scores",
    )(page_table, q2, w2, knew_t, *([kidx_t] * PAGES_PER_STEP))


def _idx_select_kernel(keys_ref, tri_ref, bias_ref, *, n_pages, k_top, t_new):
    keys = keys_ref[...]
    n_seq, n_tiles = keys.shape[:2]

    def count_where(pred):
        part = jnp.sum(jnp.where(pred(keys), 1.0, 0.0), axis=1, keepdims=True)
        return jnp.sum(part, axis=-1, keepdims=True)

    t_row = lax.broadcasted_iota(I32, (n_seq, 1, IDX_ROWS, 1), 2)
    n_valid = (jnp.minimum(t_row, t_new - 1) + (n_pages * PAGE_SIZE + 1)).astype(F32)
    thr = _select_threshold(lambda t: count_where(lambda kv: kv >= t), n_valid, float(k_top))
    need = float(k_top) - count_where(lambda kv: kv > thr)

    eq = jnp.where(keys == thr, 1.0, 0.0).astype(MXU_DTYPE)
    pref = _dot(eq.reshape(n_seq * n_tiles * IDX_ROWS, LANES), tri_ref[...]).reshape(keys.shape)
    running = jnp.zeros((n_seq, 1, IDX_ROWS, 1), F32)
    for t in range(n_tiles):
        kv = keys[:, t:t + 1]
        pt = pref[:, t:t + 1]
        take_eq = jnp.where(running + pt <= need, 0.0, NEG_BIAS)
        bias_ref[:, t:t + 1] = jnp.where(kv > thr, 0.0, jnp.where(kv == thr, take_eq, NEG_BIAS))
        running = running + pt[..., LANES - 1:LANES]


def _idx_select_call(keys, n_pages, k_top, t_new):
    db, n_tiles = keys.shape[:2]
    seqs = min(SELECT_SEQS, db)
    tri = jnp.asarray((np.arange(LANES)[:, None] <= np.arange(LANES)[None, :]).astype(np.float32), MXU_DTYPE)
    spec = pl.BlockSpec((seqs, n_tiles, IDX_ROWS, LANES), lambda b: (b, 0, 0, 0))
    return pl.pallas_call(
        functools.partial(_idx_select_kernel, n_pages=n_pages, k_top=k_top, t_new=t_new),
        grid=(db // seqs,),
        in_specs=[spec, pl.BlockSpec((LANES, LANES), lambda b: (0, 0))],
        out_specs=spec,
        out_shape=jax.ShapeDtypeStruct(keys.shape, F32),
        compiler_params=pltpu.CompilerParams(
            dimension_semantics=("arbitrary",), vmem_limit_bytes=VMEM_LIMIT),
        name="dsa_sample_select",
    )(keys, tri)


def _paged_attn_kernel(pt_ref, q_ref, bias_ref, bias_new_ref, knew_ref, vnew_ref, *rest, t_new):
    kpages = rest[:PAGES_PER_STEP]
    vpages = rest[PAGES_PER_STEP:2 * PAGES_PER_STEP]
    out_ref, m_sc, l_sc, acc_sc = rest[2 * PAGES_PER_STEP:]
    i = pl.program_id(1)

    @pl.when(i == 0)
    def _():
        m_sc[...] = jnp.full(m_sc.shape, NEG_BIAS, F32)
        l_sc[...] = jnp.zeros(l_sc.shape, F32)
        acc_sc[...] = jnp.zeros(acc_sc.shape, F32)

    q = q_ref[0]

    def expand(b8):
        return jnp.concatenate(
            [jnp.broadcast_to(b8[t:t + 1, :], (N_HEADS_A, LANES)) for t in range(t_new)], axis=0)

    def step(k_tiles, v_tiles, biases):
        s = jnp.concatenate([_dot(q, kt) * HEAD_DIM_A ** -0.5 + expand(bb)
                             for kt, bb in zip(k_tiles, biases)], axis=1)
        m = m_sc[...]
        m_new = jnp.maximum(m, jnp.max(s, axis=-1, keepdims=True))
        alpha = jnp.exp(m - m_new)
        p = jnp.exp(s - m_new)
        l_sc[...] = alpha * l_sc[...] + jnp.sum(p, axis=-1, keepdims=True)
        pv = _dot_nt(p[:, :LANES].astype(MXU_DTYPE), v_tiles[0])
        for n, vt in enumerate(v_tiles[1:], 1):
            pv = pv + _dot_nt(p[:, n * LANES:(n + 1) * LANES].astype(MXU_DTYPE), vt)
        acc_sc[...] = alpha * acc_sc[...] + pv
        m_sc[...] = m_new

    step([kp[...].astype(MXU_DTYPE) for kp in kpages], [vp[...].astype(MXU_DTYPE) for vp in vpages],
         [bias_ref[0, gg] for gg in range(PAGES_PER_STEP)])

    @pl.when(i == pl.num_programs(1) - 1)
    def _():
        step([knew_ref[0]], [vnew_ref[0]], [bias_new_ref[0, 0]])
        out_ref[0] = acc_sc[...] / l_sc[...]


def _paged_attn_call(page_table, qp, bias, knew_t, vnew_t, k_t, v_t, t_new):
    db, n_pages = page_table.shape
    rows = t_new * N_HEADS_A
    bspec = lambda shape: pl.BlockSpec((1,) + shape, lambda b, i, pt: (b,) + (0,) * len(shape))
    return pl.pallas_call(
        functools.partial(_paged_attn_kernel, t_new=t_new),
        grid_spec=pltpu.PrefetchScalarGridSpec(
            num_scalar_prefetch=1,
            grid=(db, n_pages // PAGES_PER_STEP),
            in_specs=[bspec((rows, KV_W)),
                      pl.BlockSpec((1, PAGES_PER_STEP, IDX_ROWS, LANES), lambda b, i, pt: (b, i, 0, 0)),
                      pl.BlockSpec((1, 1, IDX_ROWS, LANES), lambda b, i, pt: (b, n_pages, 0, 0)),
                      bspec((KV_W, PAGE_SIZE)), bspec((KV_W, PAGE_SIZE))]
                     + _page_specs(KV_W) + _page_specs(KV_W),
            out_specs=bspec((rows, KV_W)),
            scratch_shapes=[pltpu.VMEM((rows, 1), F32), pltpu.VMEM((rows, 1), F32),
                            pltpu.VMEM((rows, KV_W), F32)]),
        out_shape=jax.ShapeDtypeStruct((db, rows, KV_W), F32),
        compiler_params=pltpu.CompilerParams(dimension_semantics=("arbitrary", "arbitrary")),
        name="dsa_sample_attend",
    )(page_table, qp, bias, bias, knew_t, vnew_t, *([k_t] * PAGES_PER_STEP), *([v_t] * PAGES_PER_STEP))


def _layout_w_in(w_in):
    d = w_in.shape[0]
    cuts = np.cumsum(PROJ_SIZES)[:-1].tolist()
    wq, wk, wv, wiq, wik, wiw, wrq, wrk, wrv, wrg = jnp.split(w_in, cuts, axis=1)
    dup = lambda w_: jnp.concatenate(
        [w_[:, g * HEAD_DIM_A:(g + 1) * HEAD_DIM_A] for g in range(N_KV_A) for _ in range(2)], axis=1)
    pad_heads = lambda w_: jnp.pad(
        w_.reshape(d, N_HEADS_R, DK_R), ((0, 0), (0, 0), (0, LANES - DK_R))).reshape(d, N_HEADS_R * LANES)
    ikw = jnp.pad(jnp.concatenate([wik, wiw], axis=1), ((0, 0), (0, LANES - D_IDX - N_IDX_HEADS)))
    w = jnp.concatenate([wq, wk, wv, dup(wk), wiq, ikw, pad_heads(wrq), pad_heads(wrk), wrv, wrg], axis=1)
    assert w.shape[1] == W_PROJ
    return w.astype(MXU_DTYPE)


def _rope_tables(pos):
    posf = pos.astype(F32)[:, None]

    def tables(rot_dim, theta, period, width):
        half = rot_dim // 2
        inv = theta ** (-jnp.arange(half, dtype=F32) / half)
        ang = posf * inv[None, :]
        cos, sin = jnp.cos(ang), jnp.sin(ang)
        n = pos.shape[0]
        rest = period - rot_dim
        c = jnp.concatenate([cos, cos, jnp.ones((n, rest), F32)], axis=1)
        su = jnp.concatenate([-sin, jnp.zeros((n, half + rest), F32)], axis=1)
        sd = jnp.concatenate([jnp.zeros((n, half), F32), sin, jnp.zeros((n, rest), F32)], axis=1)
        return [jnp.tile(t, (1, width // period)) for t in (c, su, sd)]

    a = tables(ROT_DIM_A, ROPE_THETA, HEAD_DIM_A, LANES)
    r = tables(DK_R, RET_THETA, LANES, LANES)
    return jnp.concatenate(a + r, axis=1)


def kernel(x_prompt, x_sample, cache_k, cache_v, cache_kidx, state_ret, page_table, w_in, w_out, w_up, w_down,
           g_pre_mix, g_post_mix, g_pre_mlp, g_post_mlp, kidx_ln_g, kidx_ln_b, ret_gn_g, ret_gn_b):
    b, s, d = x_prompt.shape
    db, t_new, _ = x_sample.shape
    n_pages = page_table.shape[1]
    past = n_pages * PAGE_SIZE
    layer = 0
    assert w_in.shape[0] == 1 and t_new <= IDX_ROWS and n_pages % PAGES_PER_STEP == 0
    assert s % (2 * KEY_TILE) == 0 and (db * SAMPLE_ROWS) % KEY_TILE == 0

    w_proj = _layout_w_in(w_in[layer])
    wo, wu, wd = (w_[layer].astype(MXU_DTYPE) for w_ in (w_out, w_up, w_down))
    row = lambda v: v[layer][None, :].astype(F32)
    g_pre, g1, g2, g3 = row(g_pre_mix), row(g_post_mix), row(g_pre_mlp), row(g_post_mlp)
    lng = jnp.pad(row(kidx_ln_g), ((0, 0), (0, LANES - D_IDX)))
    lnb = jnp.pad(row(kidx_ln_b), ((0, 0), (0, LANES - D_IDX)))
    gng, gnb = row(ret_gn_g), row(ret_gn_b)

    tm = 2 * KEY_TILE
    pp = _proj_call(x_prompt, _rope_tables(jnp.arange(s)), g_pre, w_proj, lng, lnb, tm)
    a_out = _dsa_call(pp["qa"], pp["iq"], pp["iwT"], pp["ikd"], pp["kk"], pp["vTb"])
    zero_state = jnp.zeros((b, N_HEADS_R, DK_R, DV_R), F32)
    r_out, ret_prompt = _ret_call(pp["rq"], pp["rk"], pp["rv"], pp["rg"], zero_state, gng, gnb,
                                  RET_CHUNK, float(RET_CHUNK))
    y_prompt = _mlp_call(x_prompt.reshape(b * s, d), a_out.reshape(b * s, 512), r_out.reshape(b * s, 512),
                         wo, wu, wd, g1, g2, g3, tm).reshape(b, s, d)
    kv_out = lambda t: t.reshape(b, N_KV_A, HEAD_DIM_A, s).transpose(0, 3, 1, 2)[None]
    k_prompt, v_prompt = kv_out(pp["kT"]), kv_out(pp["vT"])
    kidx_prompt = pp["ikT"].transpose(0, 2, 1)[None]

    sr = SAMPLE_ROWS
    n_s = db * sr
    xs = jnp.pad(x_sample, ((0, 0), (0, sr - t_new), (0, 0))).reshape(1, n_s, d)
    pos_s = jnp.tile(past + jnp.arange(sr), db)
    tm_s = min(2 * KEY_TILE, n_s)
    ps = _proj_call(xs, _rope_tables(pos_s), g_pre, w_proj, lng, lnb, tm_s)
    per_seq = lambda t: t.reshape(db, sr, t.shape[-1])
    per_seq_t = lambda t: t.reshape(t.shape[1], db, sr).transpose(1, 0, 2)
    k_s, v_s, ik_s, iw_s = (per_seq_t(ps[n_]) for n_ in ("kT", "vT", "ikT", "iwT"))

    k_top = min(TOPK_MAX, (past + t_new) // 4)
    iq_s = per_seq(ps["iq"])[:, :IDX_ROWS].reshape(db, IDX_ROWS, N_IDX_HEADS, D_IDX)
    q2 = iq_s.transpose(0, 2, 1, 3).reshape(db, N_IDX_HEADS * IDX_ROWS, D_IDX)
    w2 = jnp.broadcast_to(iw_s[:, :, :IDX_ROWS].reshape(db, N_IDX_HEADS * IDX_ROWS, 1),
                          (db, N_IDX_HEADS * IDX_ROWS, LANES))
    pad_slots = lambda t: jnp.pad(t, ((0, 0), (0, 0), (0, PAGE_SIZE - sr))).astype(MXU_DTYPE)
    kidx_t = cache_kidx[layer].transpose(0, 2, 1)
    k_t = cache_k[layer].transpose(0, 2, 3, 1).reshape(-1, KV_W, PAGE_SIZE)
    v_t = cache_v[layer].transpose(0, 2, 3, 1).reshape(-1, KV_W, PAGE_SIZE)
    keys = _idx_scores_call(page_table, q2, w2, pad_slots(ik_s), kidx_t, t_new)
    bias = _idx_select_call(keys, n_pages, k_top, t_new)

    qa_s = per_seq(ps["qa"])[:, :t_new].reshape(db, t_new, N_HEADS_A, HEAD_DIM_A)
    grp = (jnp.arange(N_HEADS_A) // (N_HEADS_A // N_KV_A))[None, None, :, None, None]
    own = grp == jnp.arange(N_KV_A)[None, None, None, :, None]
    qp = jnp.where(own, qa_s[:, :, :, None, :], jnp.zeros((), MXU_DTYPE)).reshape(db, t_new * N_HEADS_A, KV_W)
    o_s = _paged_attn_call(page_table, qp, bias, pad_slots(k_s), pad_slots(v_s), k_t, v_t, t_new)
    o_s = o_s.reshape(db, t_new, N_HEADS_A, N_KV_A, HEAD_DIM_A)
    a_s = jnp.where(own, o_s, 0.0).sum(axis=3)
    a_s = jnp.pad(a_s.reshape(db, t_new, 512), ((0, 0), (0, sr - t_new), (0, 0))).astype(MXU_DTYPE)

    r_s, ret_sample = _ret_call(per_seq(ps["rq"]), per_seq(ps["rk"]), per_seq(ps["rv"]), per_seq(ps["rg"]),
                                state_ret[layer].astype(F32), gng, gnb, sr, float(t_new))
    y_s = _mlp_call(xs.reshape(n_s, d), a_s.reshape(n_s, 512), r_s.reshape(n_s, 512),
                    wo, wu, wd, g1, g2, g3, tm_s).reshape(db, sr, d)[:, :t_new]

    kv_s_out = lambda t: t[:, :, :t_new].reshape(db, N_KV_A, HEAD_DIM_A, t_new).transpose(0, 3, 1, 2)[None]
    return (y_prompt, y_s, k_prompt, v_prompt, kidx_prompt, ret_prompt[None],
            kv_s_out(k_s), kv_s_out(v_s), ik_s[:, :, :t_new].transpose(0, 2, 1)[None], ret_sample[None])
```

```python
import functools

import numpy as np
import jax
import jax.numpy as jnp
from jax import lax
from jax.experimental import pallas as pl
from jax.experimental.pallas import tpu as pltpu

F32 = jnp.float32
I32 = jnp.int32
MXU_DTYPE = jnp.bfloat16

LANES = 128
SUBLANES = 8
N_HEADS_A = 8
N_KV_A = 2
HEAD_DIM_A = 64
ROT_DIM_A = 16
ROPE_THETA = 500000.0
N_IDX_HEADS = 8
D_IDX = 64
TOPK_MAX = 256
N_HEADS_R = 4
DV_R = 128
DK_R = 64
RET_THETA = 10000.0
RET_CHUNK = 128
PAGE_SIZE = 128
EPS = 1e-6
PROJ_SIZES = (N_HEADS_A * HEAD_DIM_A, N_KV_A * HEAD_DIM_A, N_KV_A * HEAD_DIM_A,
              N_IDX_HEADS * D_IDX, D_IDX, N_IDX_HEADS,
              N_HEADS_R * DK_R, N_HEADS_R * DK_R, N_HEADS_R * DV_R, N_HEADS_R * DV_R)
KV_W = N_KV_A * HEAD_DIM_A

INT_MIN = -(2 ** 31)
NEG_BIAS = -1e30
SAMPLE_ROWS = 16
KEY_TILE = 256
SEARCH_BITS_PER_CHECK = 4
VMEM_LIMIT = 56 * 1024 * 1024

_SEG = {}
_off = 0
for _name, _w in (("q", 512), ("k", 128), ("v", 128), ("iq", 512),
                  ("ikw", 128), ("rq", 256), ("rk", 256), ("rv", 512), ("rg", 512)):
    _SEG[_name] = (_off, _off + _w)
    _off += _w
W_PROJ = _off


def _dot(a, b):
    return jnp.dot(a, b, preferred_element_type=F32)


def _dot_nt(a, b):
    return lax.dot_general(a, b, (((1,), (1,)), ((), ())), preferred_element_type=F32)


def _dot_tn(a, b):
    return lax.dot_general(a, b, (((0,), (0,)), ((), ())), preferred_element_type=F32)


def _tile_lanes(t, width):
    reps = width // t.shape[1]
    return t if reps == 1 else jnp.concatenate([t] * reps, axis=1)


def _sortable_key(score):
    score = jnp.where(score == 0.0, 0.0, score)
    bits = pltpu.bitcast(score, I32)
    return jnp.where(bits >= 0, bits, bits ^ 0x7FFFFFFF)


def _select_threshold(count_ge, n_valid, k_top):
    def cond(c):
        bit, _, cnt = c
        return jnp.logical_and(bit >= 0, jnp.max(cnt) > k_top)

    def step(_, c):
        bit, u, cnt = c
        cand = u | lax.shift_left(jnp.int32(1), bit)
        c_new = count_ge(cand ^ INT_MIN)
        ok = c_new >= k_top
        return bit - 1, jnp.where(ok, cand, u), jnp.where(ok, c_new, cnt)

    def body(c):
        return lax.fori_loop(0, SEARCH_BITS_PER_CHECK, step, c)

    u0 = jnp.zeros(n_valid.shape, I32)
    _, u, cnt = lax.while_loop(cond, body, (jnp.int32(31), u0, n_valid))
    return jnp.maximum(u ^ INT_MIN, INT_MIN + 1), cnt


def _rope(z, cos_t, sin_up_t, sin_dn_t, half):
    w = z.shape[1]
    up = pltpu.roll(z, w - half, 1)
    dn = pltpu.roll(z, half, 1)
    return z * _tile_lanes(cos_t, w) + up * _tile_lanes(sin_up_t, w) + dn * _tile_lanes(sin_dn_t, w)


def _proj_kernel(x_ref, tab_ref, g_ref, w_ref, lng_ref, lnb_ref,
                 qa_o, kb_o, iq_o, ikd_o, rq_o, rk_o, rv_o, rg_o, kt_o, vt_o, vtb_o, ikt_o, iwt_o):
    x = x_ref[0]
    h = (x * lax.rsqrt(jnp.mean(x * x, axis=-1, keepdims=True) + EPS) * g_ref[...]).astype(MXU_DTYPE)

    def seg(name):
        lo, hi = _SEG[name]
        return _dot(h, w_ref[:, lo:hi])

    tab = tab_ref[...]
    c_a, su_a, sd_a, c_r, su_r, sd_r = [tab[:, i * LANES:(i + 1) * LANES] for i in range(6)]
    half_a = ROT_DIM_A // 2
    half_r = DK_R // 2

    qa_o[0] = _rope(seg("q"), c_a, su_a, sd_a, half_a).astype(qa_o.dtype)
    iq_o[0] = _rope(seg("iq"), c_a, su_a, sd_a, half_a).astype(iq_o.dtype)
    rq_o[0] = _rope(seg("rq"), c_r, su_r, sd_r, half_r).astype(rq_o.dtype)
    rk_o[0] = _rope(seg("rk") * DK_R ** -0.5, c_r, su_r, sd_r, half_r).astype(rk_o.dtype)
    rv_o[0] = seg("rv").astype(rv_o.dtype)
    rg_o[0] = seg("rg")

    k = _rope(seg("k"), c_a, su_a, sd_a, half_a)
    kb_o[0] = k.astype(kb_o.dtype)
    kt_o[0] = k.T
    vt = seg("v").T
    vt_o[0] = vt
    for t in range(vtb_o.shape[1]):
        vtb_o[0, t] = vt[:, t * KEY_TILE:(t + 1) * KEY_TILE].astype(vtb_o.dtype)

    zi = seg("ikw")
    lane = lax.broadcasted_iota(I32, zi.shape, 1)
    is_ik = lane < D_IDX
    mu = jnp.sum(jnp.where(is_ik, zi, 0.0), axis=-1, keepdims=True) * (1.0 / D_IDX)
    d = jnp.where(is_ik, zi - mu, 0.0)
    var = jnp.sum(d * d, axis=-1, keepdims=True) * (1.0 / D_IDX)
    y = d * lax.rsqrt(var + EPS) * lng_ref[...] + lnb_ref[...]
    ikr = jnp.where(is_ik, _rope(y, c_a, su_a, sd_a, half_a), 0.0)
    ikd_o[0] = (ikr + pltpu.roll(ikr, D_IDX, 1)).astype(ikd_o.dtype)
    both_t = jnp.where(is_ik, ikr, zi * (N_IDX_HEADS * D_IDX) ** -0.5).T
    ikt_o[0] = both_t[:D_IDX]
    iwt_o[0] = both_t[D_IDX:D_IDX + N_IDX_HEADS]


def _proj_call(x3, tab, g, w, lng, lnb, tm):
    bx, n, d = x3.shape
    grid = (n // tm, bx)
    tspec = lambda w_: pl.BlockSpec((1, tm, w_), lambda i, b: (b, i, 0))
    ttspec = lambda r_: pl.BlockSpec((1, r_, tm), lambda i, b: (b, 0, i))
    const = lambda shape: pl.BlockSpec(shape, lambda i, b: (0,) * len(shape))
    nat = (("qa", 512, MXU_DTYPE), ("kb", KV_W, MXU_DTYPE), ("iq", 512, MXU_DTYPE), ("ikd", 128, MXU_DTYPE),
           ("rq", 256, MXU_DTYPE), ("rk", 256, MXU_DTYPE), ("rv", 512, MXU_DTYPE), ("rg", 512, F32))
    tr = (("kT", KV_W, F32), ("vT", KV_W, F32))
    tr2 = (("ikT", D_IDX, F32), ("iwT", N_IDX_HEADS, F32))
    out_specs = ([tspec(w_) for _, w_, _ in nat] + [ttspec(r_) for _, r_, _ in tr]
                 + [pl.BlockSpec((1, tm // KEY_TILE, KV_W, KEY_TILE), lambda i, b: (b, i, 0, 0))]
                 + [ttspec(r_) for _, r_, _ in tr2])
    out_shape = ([jax.ShapeDtypeStruct((bx, n, w_), dt) for _, w_, dt in nat]
                 + [jax.ShapeDtypeStruct((bx, r_, n), dt) for _, r_, dt in tr]
                 + [jax.ShapeDtypeStruct((bx, n // KEY_TILE, KV_W, KEY_TILE), MXU_DTYPE)]
                 + [jax.ShapeDtypeStruct((bx, r_, n), dt) for _, r_, dt in tr2])
    names = [o[0] for o in nat] + [o[0] for o in tr] + ["vTb"] + [o[0] for o in tr2]
    res = pl.pallas_call(
        _proj_kernel,
        grid=grid,
        in_specs=[tspec(d),
                  pl.BlockSpec((tm, 6 * LANES), lambda i, b: (i, 0)),
                  const((1, d)), const((d, W_PROJ)), const((1, LANES)), const((1, LANES))],
        out_specs=tuple(out_specs),
        out_shape=tuple(out_shape),
        compiler_params=pltpu.CompilerParams(
            dimension_semantics=("arbitrary", "arbitrary"), vmem_limit_bytes=VMEM_LIMIT),
        name="proj",
    )(x3, tab, g, w, lng, lnb)
    return dict(zip(names, res))


def _dsa_kernel(qa_ref, iq_ref, iwt_ref, ikd_ref, kb_ref, vtb_ref, tri_ref, out_ref,
                keys_sc, bias_sc, iqp_sc, qap_sc, acc_sc, *, k_top):
    tq = tk = KEY_TILE
    j = pl.program_id(1)
    q0 = j * tq
    nkt = j + 1
    heads_per_kv = N_HEADS_A // N_KV_A

    lane = lax.broadcasted_iota(I32, (tq, LANES), 1)
    lo_half = lane < HEAD_DIM_A
    hi_half = jnp.logical_not(lo_half)
    for h in range(N_HEADS_A):
        sl = slice((h // 2) * LANES, (h // 2 + 1) * LANES)
        iqp_sc[h] = jnp.where(lo_half if h % 2 == 0 else hi_half, iq_ref[0, :, sl], jnp.zeros((), MXU_DTYPE))
        g = h // heads_per_kv
        q = qa_ref[0, :, sl].astype(F32) * HEAD_DIM_A ** -0.5
        if h % 2 != g:
            q = pltpu.roll(q, HEAD_DIM_A, 1)
        qap_sc[h] = jnp.where(lo_half if g == 0 else hi_half, q, 0.0).astype(MXU_DTYPE)

    kpos = lax.broadcasted_iota(I32, (tk, tq), 0)
    qpos = lax.broadcasted_iota(I32, (tk, tq), 1) + q0

    def score_body(kt, carry):
        k0 = pl.multiple_of(kt * tk, tk)
        kb = ikd_ref[0, pl.ds(k0, tk), :]
        acc = jnp.zeros((tk, tq), F32)
        for h in range(N_IDX_HEADS):
            acc = acc + jnp.maximum(_dot_nt(kb, iqp_sc[h]), 0.0) * iwt_ref[0, h:h + 1, :]
        keys_sc[kt] = jnp.where(kpos + k0 <= qpos, _sortable_key(acc), INT_MIN)
        return carry

    lax.fori_loop(0, nkt, score_body, 0)

    def count_where(pred):
        def body(kt, part):
            hit = jnp.where(pred(keys_sc[kt]), 1.0, 0.0)
            return part + jnp.sum(hit.reshape(tk // SUBLANES, SUBLANES, tq), axis=0)
        part = lax.fori_loop(0, nkt, body, jnp.zeros((SUBLANES, tq), F32))
        return jnp.sum(part, axis=0, keepdims=True)

    n_valid = (lax.broadcasted_iota(I32, (1, tq), 1) + (q0 + 1)).astype(F32)
    thr, n_ge = _select_threshold(lambda t: count_where(lambda kv: kv >= t), n_valid, float(k_top))
    has_ties = jnp.max(n_ge) > float(k_top)

    @pl.when(jnp.logical_not(has_ties))
    def _():
        def bias_body(kt, carry):
            bias_sc[kt] = jnp.where(keys_sc[kt] >= thr, 0.0, NEG_BIAS)
            return carry
        lax.fori_loop(0, nkt, bias_body, 0)

    @pl.when(has_ties)
    def _():
        need = float(k_top) - count_where(lambda kv: kv > thr)

        def bias_body(kt, running):
            kv = keys_sc[kt]
            eq = kv == thr
            pref = _dot(tri_ref[...], jnp.where(eq, 1.0, 0.0).astype(MXU_DTYPE))
            take_eq = jnp.where(running + pref <= need, 0.0, NEG_BIAS)
            bias_sc[kt] = jnp.where(kv > thr, 0.0, jnp.where(eq, take_eq, NEG_BIAS))
            return running + pref[tk - 1:tk, :]

        lax.fori_loop(0, nkt, bias_body, jnp.zeros((1, tq), F32))

    wide = heads_per_kv * tq
    acc_sc[...] = jnp.zeros(acc_sc.shape, F32)

    def att_body(kt, carry):
        k0 = pl.multiple_of(kt * tk, tk)
        bias = _tile_lanes(bias_sc[kt], wide)
        kb = kb_ref[0, pl.ds(k0, tk), :]
        out = []
        for g, (m, l) in enumerate(carry):
            qg = qap_sc[g * heads_per_kv:(g + 1) * heads_per_kv].reshape(wide, LANES)
            vb = vtb_ref[0, kt, g * HEAD_DIM_A:(g + 1) * HEAD_DIM_A, :]
            s = _dot_nt(kb, qg) + bias
            m_new = jnp.maximum(m, jnp.max(s, axis=0, keepdims=True))
            alpha = jnp.exp(m - m_new)
            p = jnp.exp(s - m_new)
            acc_sc[g] = alpha * acc_sc[g] + _dot(vb, p.astype(MXU_DTYPE))
            psum = jnp.sum(p.reshape(tk // SUBLANES, SUBLANES, wide), axis=0)
            out.append((m_new, alpha * l + psum))
        return tuple(out)

    init = tuple((jnp.full((1, wide), NEG_BIAS, F32), jnp.zeros((SUBLANES, wide), F32)) for _ in range(N_KV_A))
    fin = lax.fori_loop(0, nkt, att_body, init)
    for g, (_, l) in enumerate(fin):
        acc_sc[g] = acc_sc[g] / jnp.sum(l, axis=0, keepdims=True)

    for hp in range(N_HEADS_A // 2):
        g, n = divmod(2 * hp, heads_per_kv)
        pair = jnp.concatenate([acc_sc[g, :, n * tq:(n + 1) * tq],
                                acc_sc[g, :, (n + 1) * tq:(n + 2) * tq]], axis=0)
        out_ref[0, :, hp * LANES:(hp + 1) * LANES] = pair.T.astype(out_ref.dtype)


def _dsa_call(qa, iq, iwt, ikd, kb, vtb):
    b, s, _ = qa.shape
    tq = KEY_TILE
    nq = s // tq
    k_top = min(TOPK_MAX, s // 4)
    tri = jnp.asarray((np.arange(tq)[:, None] >= np.arange(tq)[None, :]).astype(np.float32), MXU_DTYPE)
    qspec = lambda w_: pl.BlockSpec((1, tq, w_), lambda bb, jj: (bb, jj, 0))
    sspec = lambda w_: pl.BlockSpec((1, s, w_), lambda bb, jj: (bb, 0, 0))
    return pl.pallas_call(
        functools.partial(_dsa_kernel, k_top=k_top),
        grid=(b, nq),
        in_specs=[qspec(512), qspec(512),
                  pl.BlockSpec((1, N_IDX_HEADS, tq), lambda bb, jj: (bb, 0, jj)),
                  sspec(128), sspec(KV_W),
                  pl.BlockSpec((1, nq, KV_W, tq), lambda bb, jj: (bb, 0, 0, 0)),
                  pl.BlockSpec((tq, tq), lambda bb, jj: (0, 0))],
        out_specs=qspec(512),
        out_shape=jax.ShapeDtypeStruct((b, s, 512), MXU_DTYPE),
        scratch_shapes=[pltpu.VMEM((nq, tq, tq), I32), pltpu.VMEM((nq, tq, tq), F32),
                        pltpu.VMEM((N_IDX_HEADS, tq, LANES), MXU_DTYPE),
                        pltpu.VMEM((N_HEADS_A, tq, LANES), MXU_DTYPE),
                        pltpu.VMEM((N_KV_A, HEAD_DIM_A, (N_HEADS_A // N_KV_A) * tq), F32)],
        compiler_params=pltpu.CompilerParams(
            dimension_semantics=("arbitrary", "arbitrary"), vmem_limit_bytes=VMEM_LIMIT),
        name="dsa_prompt",
    )(qa, iq, iwt, ikd, kb, vtb, tri)


def _ret_kernel(rq_ref, rk_ref, rv_ref, rg_ref, st_ref, dmask_ref, cross_ref, kdec_ref, gdec_ref,
                gng_ref, gnb_ref, out_ref, st_out_ref, st_sc):
    c = pl.program_id(1)

    @pl.when(c == 0)
    def _():
        st_sc[...] = jnp.zeros(st_sc.shape, F32)
        for h in range(N_HEADS_R):
            off = (h % 2) * DK_R
            st_sc[h, off:off + DK_R, :] = st_ref[0, h]

    lane = lax.broadcasted_iota(I32, (rq_ref.shape[1], LANES), 1)
    for h in range(N_HEADS_R):
        sl = slice(h * LANES, (h + 1) * LANES)
        psl = slice((h // 2) * LANES, (h // 2 + 1) * LANES)
        own = (lane < DK_R) if h % 2 == 0 else (lane >= DK_R)
        q = jnp.where(own, rq_ref[0, :, psl], jnp.zeros((), MXU_DTYPE))
        k = jnp.where(own, rk_ref[0, :, psl], jnp.zeros((), MXU_DTYPE))
        v = rv_ref[0, :, sl]
        state = st_sc[h]
        inner = _dot_nt(q, k) * dmask_ref[h]
        o = _dot(inner.astype(MXU_DTYPE), v) + _dot(q, state.astype(MXU_DTYPE)) * cross_ref[:, sl]
        kd = (k.astype(F32) * kdec_ref[:, psl]).astype(MXU_DTYPE)
        st_sc[h] = state * gdec_ref[h] + _dot_tn(kd, v)

        mu = jnp.mean(o, axis=-1, keepdims=True)
        d = o - mu
        var = jnp.mean(d * d, axis=-1, keepdims=True)
        y = d * lax.rsqrt(var + EPS) * gng_ref[:, sl] + gnb_ref[:, sl]
        gate = rg_ref[0, :, sl]
        out_ref[0, :, sl] = (gate * (1.0 / (1.0 + jnp.exp(-gate))) * y).astype(out_ref.dtype)

    @pl.when(c == pl.num_programs(1) - 1)
    def _():
        for h in range(N_HEADS_R):
            off = (h % 2) * DK_R
            st_out_ref[0, h] = st_sc[h, off:off + DK_R, :]


def _ret_tables(cr, c_eff):
    log_g = jnp.log(1.0 - 2.0 ** (-5.0 - jnp.arange(N_HEADS_R, dtype=F32)))
    i = jnp.arange(cr, dtype=F32)
    diff = i[:, None] - i[None, :]
    dmask = jnp.where(diff >= 0, jnp.exp(jnp.maximum(diff, 0.0)[None] * log_g[:, None, None]), 0.0)
    cross = jnp.exp((i + 1.0)[:, None] * log_g[None, :])
    kdec = jnp.where((i < c_eff)[:, None], jnp.exp((c_eff - 1.0 - i)[:, None] * log_g[None, :]), 0.0)
    gdec = jnp.exp(c_eff * log_g)
    return (dmask, jnp.repeat(cross, DV_R, axis=1), jnp.repeat(kdec, DK_R, axis=1),
            jnp.broadcast_to(gdec[:, None, None], (N_HEADS_R, 1, LANES)))


def _ret_call(rq, rk, rv, rg, state, gng, gnb, cr, c_eff):
    b, s, _ = rq.shape
    nc = s // cr
    dmask, cross, kdec, gdec = _ret_tables(cr, c_eff)
    kw = N_HEADS_R * DK_R
    tspec = pl.BlockSpec((1, cr, 512), lambda bb, cc: (bb, cc, 0))
    kspec = pl.BlockSpec((1, cr, kw), lambda bb, cc: (bb, cc, 0))
    sspec = pl.BlockSpec((1, N_HEADS_R, DK_R, DV_R), lambda bb, cc: (bb, 0, 0, 0))
    const = lambda shape: pl.BlockSpec(shape, lambda bb, cc: (0,) * len(shape))
    return pl.pallas_call(
        _ret_kernel,
        grid=(b, nc),
        in_specs=[kspec, kspec, tspec, tspec, sspec,
                  const((N_HEADS_R, cr, cr)), const((cr, 512)), const((cr, kw)),
                  const((N_HEADS_R, 1, LANES)), const((1, 512)), const((1, 512))],
        out_specs=(tspec, sspec),
        out_shape=(jax.ShapeDtypeStruct((b, s, 512), MXU_DTYPE),
                   jax.ShapeDtypeStruct((b, N_HEADS_R, DK_R, DV_R), F32)),
        scratch_shapes=[pltpu.VMEM((N_HEADS_R, LANES, DV_R), F32)],
        compiler_params=pltpu.CompilerParams(dimension_semantics=("arbitrary", "arbitrary")),
        name="retention",
    )(rq, rk, rv, rg, state, dmask, cross, kdec, gdec, gng, gnb)


def _mlp_kernel(x_ref, a_ref, r_ref, wo_ref, wu_ref, wd_ref, g1_ref, g2_ref, g3_ref, out_ref, *, ff_chunk):
    def rms(t, g_ref):
        return t * lax.rsqrt(jnp.mean(t * t, axis=-1, keepdims=True) + EPS) * g_ref[...]

    half = a_ref.shape[1]
    mix = _dot(a_ref[...], wo_ref[:half, :]) + _dot(r_ref[...], wo_ref[half:, :])
    x1 = x_ref[...] + rms(mix, g1_ref)
    h2 = rms(x1, g2_ref).astype(MXU_DTYPE)
    acc = jnp.zeros(x1.shape, F32)
    for c in range(wu_ref.shape[1] // ff_chunk):
        sl = slice(c * ff_chunk, (c + 1) * ff_chunk)
        u = jnp.maximum(_dot(h2, wu_ref[:, sl]), 0.0)
        acc = acc + _dot((u * u).astype(MXU_DTYPE), wd_ref[sl, :])
    out_ref[...] = x1 + rms(acc, g3_ref)


def _mlp_call(x2, a, r, wo, wu, wd, g1, g2, g3, tm):
    n, d = x2.shape
    dff = wu.shape[1]
    tspec = lambda w_: pl.BlockSpec((tm, w_), lambda i: (i, 0))
    const = lambda shape: pl.BlockSpec(shape, lambda i: (0, 0))
    return pl.pallas_call(
        functools.partial(_mlp_kernel, ff_chunk=1024),
        grid=(n // tm,),
        in_specs=[tspec(d), tspec(512), tspec(512), const((d, d)), const((d, dff)), const((dff, d)),
                  const((1, d)), const((1, d)), const((1, d))],
        out_specs=tspec(d),
        out_shape=jax.ShapeDtypeStruct((n, d), F32),
        compiler_params=pltpu.CompilerParams(
            dimension_semantics=("arbitrary",), vmem_limit_bytes=VMEM_LIMIT),
        name="mix_mlp",
    )(x2, a, r, wo, wu, wd, g1, g2, g3)


PAGES_PER_STEP = 16
IDX_ROWS = 8
SELECT_SEQS = 16


def _page_specs(n_feat):
    return [pl.BlockSpec((None, n_feat, PAGE_SIZE),
                         functools.partial(lambda b, i, pt, gg: (pt[b, i * PAGES_PER_STEP + gg], 0, 0), gg=gg))
            for gg in range(PAGES_PER_STEP)]


def _idx_scores_kernel(pt_ref, q_ref, w_ref, knew_ref, *rest, n_pages, t_new):
    pages = rest[:PAGES_PER_STEP]
    keys_ref = rest[PAGES_PER_STEP]
    i = pl.program_id(1)
    q = q_ref[0]
    w = w_ref[0]

    def scores(kpage_t):
        s = jnp.maximum(_dot(q, kpage_t), 0.0) * w
        acc = s[0:IDX_ROWS]
        for h in range(1, N_IDX_HEADS):
            acc = acc + s[h * IDX_ROWS:(h + 1) * IDX_ROWS]
        return _sortable_key(acc)

    for gg in range(PAGES_PER_STEP):
        keys_ref[0, i * PAGES_PER_STEP + gg] = scores(pages[gg][...].astype(MXU_DTYPE))

    @pl.when(i == pl.num_programs(1) - 1)
    def _():
        row = lax.broadcasted_iota(I32, (IDX_ROWS, LANES), 0)
        col = lax.broadcasted_iota(I32, (IDX_ROWS, LANES), 1)
        new_ok = jnp.logical_and(col <= row, col < t_new)
        keys_ref[0, n_pages] = jnp.where(new_ok, scores(knew_ref[0]), INT_MIN)


def _idx_scores_call(page_table, q2, w2, knew_t, kidx_t, t_new):
    db, n_pages = page_table.shape
    n_tiles = n_pages + 1
    bspec = lambda shape: pl.BlockSpec((1,) + shape, lambda b, i, pt: (b,) + (0,) * len(shape))
    return pl.pallas_call(
        functools.partial(_idx_scores_kernel, n_pages=n_pages, t_new=t_new),
        grid_spec=pltpu.PrefetchScalarGridSpec(
            num_scalar_prefetch=1,
            grid=(db, n_pages // PAGES_PER_STEP),
            in_specs=[bspec((N_IDX_HEADS * IDX_ROWS, D_IDX)), bspec((N_IDX_HEADS * IDX_ROWS, LANES)),
                      bspec((D_IDX, PAGE_SIZE))] + _page_specs(D_IDX),
            out_specs=bspec((n_tiles, IDX_ROWS, LANES))),
        out_shape=jax.ShapeDtypeStruct((db, n_tiles, IDX_ROWS, LANES), I32),
        compiler_params=pltpu.CompilerParams(dimension_semantics=("arbitrary", "arbitrary")),
        name="dsa_sample_scores",
    )(page_table, q2, w2, knew_t, *([kidx_t] * PAGES_PER_STEP))


def _idx_select_kernel(keys_ref, tri_ref, bias_ref, *, n_pages, k_top, t_new):
    keys = keys_ref[...]
    n_seq, n_tiles = keys.shape[:2]

    def count_where(pred):
        part = jnp.sum(jnp.where(pred(keys), 1.0, 0.0), axis=1, keepdims=True)
        return jnp.sum(part, axis=-1, keepdims=True)

    t_row = lax.broadcasted_iota(I32, (n_seq, 1, IDX_ROWS, 1), 2)
    n_valid = (jnp.minimum(t_row, t_new - 1) + (n_pages * PAGE_SIZE + 1)).astype(F32)
    thr, _ = _select_threshold(lambda t: count_where(lambda kv: kv >= t), n_valid, float(k_top))
    need = float(k_top) - count_where(lambda kv: kv > thr)

    eq = jnp.where(keys == thr, 1.0, 0.0).astype(MXU_DTYPE)
    pref = _dot(eq.reshape(n_seq * n_tiles * IDX_ROWS, LANES), tri_ref[...]).reshape(keys.shape)
    running = jnp.zeros((n_seq, 1, IDX_ROWS, 1), F32)
    for t in range(n_tiles):
        kv = keys[:, t:t + 1]
        pt = pref[:, t:t + 1]
        take_eq = jnp.where(running + pt <= need, 0.0, NEG_BIAS)
        bias_ref[:, t:t + 1] = jnp.where(kv > thr, 0.0, jnp.where(kv == thr, take_eq, NEG_BIAS))
        running = running + pt[..., LANES - 1:LANES]


def _idx_select_call(keys, n_pages, k_top, t_new):
    db, n_tiles = keys.shape[:2]
    seqs = min(SELECT_SEQS, db)
    tri = jnp.asarray((np.arange(LANES)[:, None] <= np.arange(LANES)[None, :]).astype(np.float32), MXU_DTYPE)
    spec = pl.BlockSpec((seqs, n_tiles, IDX_ROWS, LANES), lambda b: (b, 0, 0, 0))
    return pl.pallas_call(
        functools.partial(_idx_select_kernel, n_pages=n_pages, k_top=k_top, t_new=t_new),
        grid=(db // seqs,),
        in_specs=[spec, pl.BlockSpec((LANES, LANES), lambda b: (0, 0))],
        out_specs=spec,
        out_shape=jax.ShapeDtypeStruct(keys.shape, F32),
        compiler_params=pltpu.CompilerParams(
            dimension_semantics=("arbitrary",), vmem_limit_bytes=VMEM_LIMIT),
        name="dsa_sample_select",
    )(keys, tri)


def _paged_attn_kernel(pt_ref, q_ref, bias_ref, bias_new_ref, knew_ref, vnew_ref, *rest, t_new):
    kpages = rest[:PAGES_PER_STEP]
    vpages = rest[PAGES_PER_STEP:2 * PAGES_PER_STEP]
    out_ref, m_sc, l_sc, acc_sc = rest[2 * PAGES_PER_STEP:]
    i = pl.program_id(1)

    @pl.when(i == 0)
    def _():
        m_sc[...] = jnp.full(m_sc.shape, NEG_BIAS, F32)
        l_sc[...] = jnp.zeros(l_sc.shape, F32)
        acc_sc[...] = jnp.zeros(acc_sc.shape, F32)

    q = q_ref[0]

    def expand(b8):
        return jnp.concatenate(
            [jnp.broadcast_to(b8[t:t + 1, :], (N_HEADS_A, LANES)) for t in range(t_new)], axis=0)

    def step(k_tiles, v_tiles, biases):
        s = jnp.concatenate([_dot(q, kt) * HEAD_DIM_A ** -0.5 + expand(bb)
                             for kt, bb in zip(k_tiles, biases)], axis=1)
        m = m_sc[...]
        m_new = jnp.maximum(m, jnp.max(s, axis=-1, keepdims=True))
        alpha = jnp.exp(m - m_new)
        p = jnp.exp(s - m_new)
        l_sc[...] = alpha * l_sc[...] + jnp.sum(p, axis=-1, keepdims=True)
        pv = _dot_nt(p[:, :LANES].astype(MXU_DTYPE), v_tiles[0])
        for n, vt in enumerate(v_tiles[1:], 1):
            pv = pv + _dot_nt(p[:, n * LANES:(n + 1) * LANES].astype(MXU_DTYPE), vt)
        acc_sc[...] = alpha * acc_sc[...] + pv
        m_sc[...] = m_new

    step([kp[...].astype(MXU_DTYPE) for kp in kpages], [vp[...].astype(MXU_DTYPE) for vp in vpages],
         [bias_ref[0, gg] for gg in range(PAGES_PER_STEP)])

    @pl.when(i == pl.num_programs(1) - 1)
    def _():
        step([knew_ref[0]], [vnew_ref[0]], [bias_new_ref[0, 0]])
        out_ref[0] = acc_sc[...] / l_sc[...]


def _paged_attn_call(page_table, qp, bias, knew_t, vnew_t, k_t, v_t, t_new):
    db, n_pages = page_table.shape
    rows = t_new * N_HEADS_A
    bspec = lambda shape: pl.BlockSpec((1,) + shape, lambda b, i, pt: (b,) + (0,) * len(shape))
    return pl.pallas_call(
        functools.partial(_paged_attn_kernel, t_new=t_new),
        grid_spec=pltpu.PrefetchScalarGridSpec(
            num_scalar_prefetch=1,
            grid=(db, n_pages // PAGES_PER_STEP),
            in_specs=[bspec((rows, KV_W)),
                      pl.BlockSpec((1, PAGES_PER_STEP, IDX_ROWS, LANES), lambda b, i, pt: (b, i, 0, 0)),
                      pl.BlockSpec((1, 1, IDX_ROWS, LANES), lambda b, i, pt: (b, n_pages, 0, 0)),
                      bspec((KV_W, PAGE_SIZE)), bspec((KV_W, PAGE_SIZE))]
                     + _page_specs(KV_W) + _page_specs(KV_W),
            out_specs=bspec((rows, KV_W)),
            scratch_shapes=[pltpu.VMEM((rows, 1), F32), pltpu.VMEM((rows, 1), F32),
                            pltpu.VMEM((rows, KV_W), F32)]),
        out_shape=jax.ShapeDtypeStruct((db, rows, KV_W), F32),
        compiler_params=pltpu.CompilerParams(dimension_semantics=("arbitrary", "arbitrary")),
        name="dsa_sample_attend",
    )(page_table, qp, bias, bias, knew_t, vnew_t, *([k_t] * PAGES_PER_STEP), *([v_t] * PAGES_PER_STEP))


def _layout_w_in(w_in):
    cuts = np.cumsum(PROJ_SIZES)[:-1].tolist()
    wq, wk, wv, wiq, wik, wiw, wrq, wrk, wrv, wrg = jnp.split(w_in, cuts, axis=1)
    ikw = jnp.pad(jnp.concatenate([wik, wiw], axis=1), ((0, 0), (0, LANES - D_IDX - N_IDX_HEADS)))
    w = jnp.concatenate([wq, wk, wv, wiq, ikw, wrq, wrk, wrv, wrg], axis=1)
    assert w.shape[1] == W_PROJ
    return w.astype(MXU_DTYPE)


def _rope_tables(pos):
    posf = pos.astype(F32)[:, None]

    def tables(rot_dim, theta, period, width):
        half = rot_dim // 2
        inv = theta ** (-jnp.arange(half, dtype=F32) / half)
        ang = posf * inv[None, :]
        cos, sin = jnp.cos(ang), jnp.sin(ang)
        n = pos.shape[0]
        rest = period - rot_dim
        c = jnp.concatenate([cos, cos, jnp.ones((n, rest), F32)], axis=1)
        su = jnp.concatenate([-sin, jnp.zeros((n, half + rest), F32)], axis=1)
        sd = jnp.concatenate([jnp.zeros((n, half), F32), sin, jnp.zeros((n, rest), F32)], axis=1)
        return [jnp.tile(t, (1, width // period)) for t in (c, su, sd)]

    a = tables(ROT_DIM_A, ROPE_THETA, HEAD_DIM_A, LANES)
    r = tables(DK_R, RET_THETA, DK_R, LANES)
    return jnp.concatenate(a + r, axis=1)


def kernel(x_prompt, x_sample, cache_k, cache_v, cache_kidx, state_ret, page_table, w_in, w_out, w_up, w_down,
           g_pre_mix, g_post_mix, g_pre_mlp, g_post_mlp, kidx_ln_g, kidx_ln_b, ret_gn_g, ret_gn_b):
    b, s, d = x_prompt.shape
    db, t_new, _ = x_sample.shape
    n_pages = page_table.shape[1]
    past = n_pages * PAGE_SIZE
    layer = 0
    assert w_in.shape[0] == 1 and t_new <= IDX_ROWS and n_pages % PAGES_PER_STEP == 0
    assert s % (2 * KEY_TILE) == 0 and (db * SAMPLE_ROWS) % KEY_TILE == 0

    w_proj = _layout_w_in(w_in[layer])
    wo, wu, wd = (w_[layer].astype(MXU_DTYPE) for w_ in (w_out, w_up, w_down))
    row = lambda v: v[layer][None, :].astype(F32)
    g_pre, g1, g2, g3 = row(g_pre_mix), row(g_post_mix), row(g_pre_mlp), row(g_post_mlp)
    lng = jnp.pad(row(kidx_ln_g), ((0, 0), (0, LANES - D_IDX)))
    lnb = jnp.pad(row(kidx_ln_b), ((0, 0), (0, LANES - D_IDX)))
    gng, gnb = row(ret_gn_g), row(ret_gn_b)

    tm = 2 * KEY_TILE
    pp = _proj_call(x_prompt, _rope_tables(jnp.arange(s)), g_pre, w_proj, lng, lnb, tm)
    a_out = _dsa_call(pp["qa"], pp["iq"], pp["iwT"], pp["ikd"], pp["kb"], pp["vTb"])
    zero_state = jnp.zeros((b, N_HEADS_R, DK_R, DV_R), F32)
    r_out, ret_prompt = _ret_call(pp["rq"], pp["rk"], pp["rv"], pp["rg"], zero_state, gng, gnb,
                                  RET_CHUNK, float(RET_CHUNK))
    y_prompt = _mlp_call(x_prompt.reshape(b * s, d), a_out.reshape(b * s, 512), r_out.reshape(b * s, 512),
                         wo, wu, wd, g1, g2, g3, tm).reshape(b, s, d)
    kv_out = lambda t: t.reshape(b, N_KV_A, HEAD_DIM_A, s).transpose(0, 3, 1, 2)[None]
    k_prompt, v_prompt = kv_out(pp["kT"]), kv_out(pp["vT"])
    kidx_prompt = pp["ikT"].transpose(0, 2, 1)[None]

    sr = SAMPLE_ROWS
    n_s = db * sr
    xs = jnp.pad(x_sample, ((0, 0), (0, sr - t_new), (0, 0))).reshape(1, n_s, d)
    pos_s = jnp.tile(past + jnp.arange(sr), db)
    tm_s = min(2 * KEY_TILE, n_s)
    ps = _proj_call(xs, _rope_tables(pos_s), g_pre, w_proj, lng, lnb, tm_s)
    per_seq = lambda t: t.reshape(db, sr, t.shape[-1])
    per_seq_t = lambda t: t.reshape(t.shape[1], db, sr).transpose(1, 0, 2)
    k_s, v_s, ik_s, iw_s = (per_seq_t(ps[n_]) for n_ in ("kT", "vT", "ikT", "iwT"))

    k_top = min(TOPK_MAX, (past + t_new) // 4)
    iq_s = per_seq(ps["iq"])[:, :IDX_ROWS].reshape(db, IDX_ROWS, N_IDX_HEADS, D_IDX)
    q2 = iq_s.transpose(0, 2, 1, 3).reshape(db, N_IDX_HEADS * IDX_ROWS, D_IDX)
    w2 = jnp.broadcast_to(iw_s[:, :, :IDX_ROWS].reshape(db, N_IDX_HEADS * IDX_ROWS, 1),
                          (db, N_IDX_HEADS * IDX_ROWS, LANES))
    pad_slots = lambda t: jnp.pad(t, ((0, 0), (0, 0), (0, PAGE_SIZE - sr))).astype(MXU_DTYPE)
    kidx_t = cache_kidx[layer].transpose(0, 2, 1)
    k_t = cache_k[layer].transpose(0, 2, 3, 1).reshape(-1, KV_W, PAGE_SIZE)
    v_t = cache_v[layer].transpose(0, 2, 3, 1).reshape(-1, KV_W, PAGE_SIZE)
    keys = _idx_scores_call(page_table, q2, w2, pad_slots(ik_s), kidx_t, t_new)
    bias = _idx_select_call(keys, n_pages, k_top, t_new)

    qa_s = per_seq(ps["qa"])[:, :t_new].reshape(db, t_new, N_HEADS_A, HEAD_DIM_A)
    grp = (jnp.arange(N_HEADS_A) // (N_HEADS_A // N_KV_A))[None, None, :, None, None]
    own = grp == jnp.arange(N_KV_A)[None, None, None, :, None]
    qp = jnp.where(own, qa_s[:, :, :, None, :], jnp.zeros((), MXU_DTYPE)).reshape(db, t_new * N_HEADS_A, KV_W)
    o_s = _paged_attn_call(page_table, qp, bias, pad_slots(k_s), pad_slots(v_s), k_t, v_t, t_new)
    o_s = o_s.reshape(db, t_new, N_HEADS_A, N_KV_A, HEAD_DIM_A)
    a_s = jnp.where(own, o_s, 0.0).sum(axis=3)
    a_s = jnp.pad(a_s.reshape(db, t_new, 512), ((0, 0), (0, sr - t_new), (0, 0))).astype(MXU_DTYPE)

    r_s, ret_sample = _ret_call(per_seq(ps["rq"]), per_seq(ps["rk"]), per_seq(ps["rv"]), per_seq(ps["rg"]),
                                state_ret[layer].astype(F32), gng, gnb, sr, float(t_new))
    y_s = _mlp_call(xs.reshape(n_s, d), a_s.reshape(n_s, 512), r_s.reshape(n_s, 512),
                    wo, wu, wd, g1, g2, g3, tm_s).reshape(db, sr, d)[:, :t_new]

    kv_s_out = lambda t: t[:, :, :t_new].reshape(db, N_KV_A, HEAD_DIM_A, t_new).transpose(0, 3, 1, 2)[None]
    return (y_prompt, y_s, k_prompt, v_prompt, kidx_prompt, ret_prompt[None],
            kv_s_out(k_s), kv_s_out(v_s), ik_s[:, :, :t_new].transpose(0, 2, 1)[None], ret_sample[None])
```

```python
import functools

import numpy as np
import jax
import jax.numpy as jnp
from jax import lax
from jax.experimental import pallas as pl
from jax.experimental.pallas import tpu as pltpu

F32 = jnp.float32
I32 = jnp.int32
MXU_DTYPE = jnp.bfloat16

LANES = 128
SUBLANES = 8
N_HEADS_A = 8
N_KV_A = 2
HEAD_DIM_A = 64
ROT_DIM_A = 16
ROPE_THETA = 500000.0
N_IDX_HEADS = 8
D_IDX = 64
TOPK_MAX = 256
N_HEADS_R = 4
DV_R = 128
DK_R = 64
RET_THETA = 10000.0
RET_CHUNK = 128
PAGE_SIZE = 128
EPS = 1e-6
PROJ_SIZES = (N_HEADS_A * HEAD_DIM_A, N_KV_A * HEAD_DIM_A, N_KV_A * HEAD_DIM_A,
              N_IDX_HEADS * D_IDX, D_IDX, N_IDX_HEADS,
              N_HEADS_R * DK_R, N_HEADS_R * DK_R, N_HEADS_R * DV_R, N_HEADS_R * DV_R)
KV_W = N_KV_A * HEAD_DIM_A

INT_MIN = -(2 ** 31)
NEG_BIAS = -1e30
SAMPLE_ROWS = 16
KEY_TILE = 256
SEARCH_BITS_PER_CHECK = 4
VMEM_LIMIT = 56 * 1024 * 1024

_SEG = {}
_off = 0
for _name, _w in (("q", 512), ("k", 128), ("v", 128), ("iq", 512),
                  ("ikw", 128), ("rq", 256), ("rk", 256), ("rv", 512), ("rg", 512)):
    _SEG[_name] = (_off, _off + _w)
    _off += _w
W_PROJ = _off


def _dot(a, b):
    return jnp.dot(a, b, preferred_element_type=F32)


def _dot_nt(a, b):
    return lax.dot_general(a, b, (((1,), (1,)), ((), ())), preferred_element_type=F32)


def _dot_tn(a, b):
    return lax.dot_general(a, b, (((0,), (0,)), ((), ())), preferred_element_type=F32)


def _tile_lanes(t, width):
    reps = width // t.shape[1]
    return t if reps == 1 else jnp.concatenate([t] * reps, axis=1)


def _sortable_key(score):
    score = jnp.where(score == 0.0, 0.0, score)
    bits = pltpu.bitcast(score, I32)
    key = jnp.where(bits >= 0, bits, bits ^ 0x7FFFFFFF)
    return key, pltpu.bitcast(bits & -65536, F32)


def _coarse_threshold(t):
    t16 = lax.shift_right_arithmetic(t, 16)
    t16 = jnp.where(jnp.logical_and(t16 > 0, t16 < 0x80), 0x80, t16)
    b16 = jnp.where(t16 >= 0, t16, t16 ^ 0x7FFF)
    return pltpu.bitcast(lax.shift_left(b16, 16), F32).astype(jnp.bfloat16)


def _select_threshold(count_ge, n_valid, k_top, count_ge_coarse=None):
    def refine(count_fn, state, last_bit):
        def cond(c):
            bit, _, cnt = c
            return jnp.logical_and(bit >= last_bit, jnp.max(cnt) > k_top)

        def step(_, c):
            bit, u, cnt = c
            cand = u | lax.shift_left(jnp.int32(1), bit)
            c_new = count_fn(cand ^ INT_MIN)
            ok = c_new >= k_top
            return bit - 1, jnp.where(ok, cand, u), jnp.where(ok, c_new, cnt)

        return lax.while_loop(cond, lambda c: lax.fori_loop(0, SEARCH_BITS_PER_CHECK, step, c), state)

    state = (jnp.int32(31), jnp.zeros(n_valid.shape, I32), n_valid)
    if count_ge_coarse is not None:
        state = refine(count_ge_coarse, state, 16)
    _, u, cnt = refine(count_ge, state, 0)
    return jnp.maximum(u ^ INT_MIN, INT_MIN + 1), cnt


def _rope(z, cos_t, sin_up_t, sin_dn_t, half):
    w = z.shape[1]
    up = pltpu.roll(z, w - half, 1)
    dn = pltpu.roll(z, half, 1)
    return z * _tile_lanes(cos_t, w) + up * _tile_lanes(sin_up_t, w) + dn * _tile_lanes(sin_dn_t, w)


def _proj_kernel(x_ref, tab_ref, g_ref, w_ref, lng_ref, lnb_ref,
                 qa_o, kb_o, iq_o, ikd_o, rq_o, rk_o, rv_o, rg_o, kt_o, vt_o, vtb_o, ikt_o, iwt_o):
    half_a = ROT_DIM_A // 2
    half_r = DK_R // 2
    for r in range(x_ref.shape[1] // KEY_TILE):
        rows = slice(r * KEY_TILE, (r + 1) * KEY_TILE)
        x = x_ref[0, rows, :]
        h = (x * lax.rsqrt(jnp.mean(x * x, axis=-1, keepdims=True) + EPS) * g_ref[...]).astype(MXU_DTYPE)

        def seg(name, h=h):
            lo, hi = _SEG[name]
            return _dot(h, w_ref[:, lo:hi])

        tab = tab_ref[rows, :]
        c_a, su_a, sd_a, c_r, su_r, sd_r = [tab[:, i * LANES:(i + 1) * LANES] for i in range(6)]

        qa_o[0, rows, :] = _rope(seg("q"), c_a, su_a, sd_a, half_a).astype(qa_o.dtype)
        iq_o[0, rows, :] = _rope(seg("iq"), c_a, su_a, sd_a, half_a).astype(iq_o.dtype)
        rq_o[0, rows, :] = _rope(seg("rq"), c_r, su_r, sd_r, half_r).astype(rq_o.dtype)
        rk_o[0, rows, :] = _rope(seg("rk") * DK_R ** -0.5, c_r, su_r, sd_r, half_r).astype(rk_o.dtype)
        rv_o[0, rows, :] = seg("rv").astype(rv_o.dtype)
        rg_o[0, rows, :] = seg("rg")

        k = _rope(seg("k"), c_a, su_a, sd_a, half_a)
        kb_o[0, rows, :] = k.astype(kb_o.dtype)
        kt_o[0, :, rows] = k.T
        vt = seg("v").T
        vt_o[0, :, rows] = vt
        vtb_o[0, r] = vt.astype(vtb_o.dtype)

        zi = seg("ikw")
        lane = lax.broadcasted_iota(I32, zi.shape, 1)
        is_ik = lane < D_IDX
        mu = jnp.sum(jnp.where(is_ik, zi, 0.0), axis=-1, keepdims=True) * (1.0 / D_IDX)
        d = jnp.where(is_ik, zi - mu, 0.0)
        var = jnp.sum(d * d, axis=-1, keepdims=True) * (1.0 / D_IDX)
        y = d * lax.rsqrt(var + EPS) * lng_ref[...] + lnb_ref[...]
        ikr = jnp.where(is_ik, _rope(y, c_a, su_a, sd_a, half_a), 0.0)
        ikd_o[0, rows, :] = (ikr + pltpu.roll(ikr, D_IDX, 1)).astype(ikd_o.dtype)
        both_t = jnp.where(is_ik, ikr, zi * (N_IDX_HEADS * D_IDX) ** -0.5).T
        ikt_o[0, :, rows] = both_t[:D_IDX]
        iwt_o[0, :, rows] = both_t[D_IDX:D_IDX + N_IDX_HEADS]


def _proj_call(x3, tab, g, w, lng, lnb, tm):
    bx, n, d = x3.shape
    grid = (n // tm, bx)
    tspec = lambda w_: pl.BlockSpec((1, tm, w_), lambda i, b: (b, i, 0))
    ttspec = lambda r_: pl.BlockSpec((1, r_, tm), lambda i, b: (b, 0, i))
    const = lambda shape: pl.BlockSpec(shape, lambda i, b: (0,) * len(shape))
    nat = (("qa", 512, MXU_DTYPE), ("kb", KV_W, MXU_DTYPE), ("iq", 512, MXU_DTYPE), ("ikd", 128, MXU_DTYPE),
           ("rq", 256, MXU_DTYPE), ("rk", 256, MXU_DTYPE), ("rv", 512, MXU_DTYPE), ("rg", 512, F32))
    tr = (("kT", KV_W, F32), ("vT", KV_W, F32))
    tr2 = (("ikT", D_IDX, F32), ("iwT", N_IDX_HEADS, F32))
    out_specs = ([tspec(w_) for _, w_, _ in nat] + [ttspec(r_) for _, r_, _ in tr]
                 + [pl.BlockSpec((1, tm // KEY_TILE, KV_W, KEY_TILE), lambda i, b: (b, i, 0, 0))]
                 + [ttspec(r_) for _, r_, _ in tr2])
    out_shape = ([jax.ShapeDtypeStruct((bx, n, w_), dt) for _, w_, dt in nat]
                 + [jax.ShapeDtypeStruct((bx, r_, n), dt) for _, r_, dt in tr]
                 + [jax.ShapeDtypeStruct((bx, n // KEY_TILE, KV_W, KEY_TILE), MXU_DTYPE)]
                 + [jax.ShapeDtypeStruct((bx, r_, n), dt) for _, r_, dt in tr2])
    names = [o[0] for o in nat] + [o[0] for o in tr] + ["vTb"] + [o[0] for o in tr2]
    res = pl.pallas_call(
        _proj_kernel,
        grid=grid,
        in_specs=[tspec(d),
                  pl.BlockSpec((tm, 6 * LANES), lambda i, b: (i, 0)),
                  const((1, d)), const((d, W_PROJ)), const((1, LANES)), const((1, LANES))],
        out_specs=tuple(out_specs),
        out_shape=tuple(out_shape),
        compiler_params=pltpu.CompilerParams(
            dimension_semantics=("arbitrary", "arbitrary"), vmem_limit_bytes=VMEM_LIMIT),
        name="proj",
    )(x3, tab, g, w, lng, lnb)
    return dict(zip(names, res))


def _dsa_kernel(qa_ref, iq_ref, iwt_ref, ikd_ref, kb_ref, vtb_ref, tri_ref, out_ref,
                keys_sc, top_sc, bias_sc, iqp_sc, qap_sc, acc_sc, *, k_top):
    tq = tk = KEY_TILE
    j = pl.program_id(1)
    q0 = j * tq
    nkt = j + 1
    heads_per_kv = N_HEADS_A // N_KV_A

    lane = lax.broadcasted_iota(I32, (tq, LANES), 1)
    lo_half = lane < HEAD_DIM_A
    hi_half = jnp.logical_not(lo_half)
    for h in range(N_HEADS_A):
        sl = slice((h // 2) * LANES, (h // 2 + 1) * LANES)
        iqp_sc[h] = jnp.where(lo_half if h % 2 == 0 else hi_half, iq_ref[0, :, sl], jnp.zeros((), MXU_DTYPE))
        g = h // heads_per_kv
        q = qa_ref[0, :, sl].astype(F32) * HEAD_DIM_A ** -0.5
        if h % 2 != g:
            q = pltpu.roll(q, HEAD_DIM_A, 1)
        qap_sc[h] = jnp.where(lo_half if g == 0 else hi_half, q, 0.0).astype(MXU_DTYPE)

    kpos = lax.broadcasted_iota(I32, (tk, tq), 0)
    qpos = lax.broadcasted_iota(I32, (tk, tq), 1) + q0

    def score_body(kt, carry):
        k0 = pl.multiple_of(kt * tk, tk)
        kb = ikd_ref[0, pl.ds(k0, tk), :]
        acc = jnp.zeros((tk, tq), F32)
        for h in range(N_IDX_HEADS):
            acc = acc + jnp.maximum(_dot_nt(kb, iqp_sc[h]), 0.0) * iwt_ref[0, h:h + 1, :]
        key, top = _sortable_key(acc)
        causal = kpos + k0 <= qpos
        keys_sc[kt] = jnp.where(causal, key, INT_MIN)
        top_sc[kt] = jnp.where(causal, top, jnp.nan).astype(top_sc.dtype)
        return carry

    lax.fori_loop(0, nkt, score_body, 0)

    def count_where(pred):
        def body(kt, part):
            hit = jnp.where(pred(keys_sc[kt]), 1.0, 0.0)
            return part + jnp.sum(hit.reshape(tk // SUBLANES, SUBLANES, tq), axis=0)
        part = lax.fori_loop(0, nkt, body, jnp.zeros((SUBLANES, tq), F32))
        return jnp.sum(part, axis=0, keepdims=True)

    def count_coarse(t):
        tb = _coarse_threshold(t)
        rows = 2 * SUBLANES

        def body(kt, part):
            hit = jnp.where(top_sc[kt] >= tb, jnp.ones((), top_sc.dtype), jnp.zeros((), top_sc.dtype))
            parts = [hit[r * rows:(r + 1) * rows] for r in range(tk // rows)]
            while len(parts) > 1:
                parts = [a + b for a, b in zip(parts[::2], parts[1::2])]
            return part + parts[0].astype(F32)
        part = lax.fori_loop(0, nkt, body, jnp.zeros((rows, tq), F32))
        return jnp.sum(part, axis=0, keepdims=True)

    n_valid = (lax.broadcasted_iota(I32, (1, tq), 1) + (q0 + 1)).astype(F32)
    thr, n_ge = _select_threshold(lambda t: count_where(lambda kv: kv >= t), n_valid, float(k_top),
                                  count_ge_coarse=count_coarse)
    has_ties = jnp.max(n_ge) > float(k_top)

    @pl.when(jnp.logical_not(has_ties))
    def _():
        def bias_body(kt, carry):
            bias_sc[kt] = jnp.where(keys_sc[kt] >= thr, 0.0, NEG_BIAS)
            return carry
        lax.fori_loop(0, nkt, bias_body, 0)

    @pl.when(has_ties)
    def _():
        need = float(k_top) - count_where(lambda kv: kv > thr)

        def bias_body(kt, running):
            kv = keys_sc[kt]
            eq = kv == thr
            pref = _dot(tri_ref[...], jnp.where(eq, 1.0, 0.0).astype(MXU_DTYPE))
            take_eq = jnp.where(running + pref <= need, 0.0, NEG_BIAS)
            bias_sc[kt] = jnp.where(kv > thr, 0.0, jnp.where(eq, take_eq, NEG_BIAS))
            return running + pref[tk - 1:tk, :]

        lax.fori_loop(0, nkt, bias_body, jnp.zeros((1, tq), F32))

    wide = heads_per_kv * tq
    acc_sc[...] = jnp.zeros(acc_sc.shape, F32)

    @pl.when(nkt % 2 == 1)
    def _():
        bias_sc[nkt] = jnp.full((tk, tq), NEG_BIAS, F32)

    def att_body(kp, carry):
        k0 = pl.multiple_of(kp * (2 * tk), 2 * tk)
        bias = _tile_lanes(bias_sc[pl.ds(2 * kp, 2)].reshape(2 * tk, tq), wide)
        kb = kb_ref[0, pl.ds(k0, 2 * tk), :]
        out = []
        for g, (m, l) in enumerate(carry):
            qg = qap_sc[g * heads_per_kv:(g + 1) * heads_per_kv].reshape(wide, LANES)
            gsl = slice(g * HEAD_DIM_A, (g + 1) * HEAD_DIM_A)
            vb = jnp.concatenate([vtb_ref[0, 2 * kp, gsl, :], vtb_ref[0, 2 * kp + 1, gsl, :]], axis=1)
            s = _dot_nt(kb, qg) + bias
            m_new = jnp.maximum(m, jnp.max(s, axis=0, keepdims=True))
            alpha = jnp.exp(m - m_new)
            p = jnp.exp(s - m_new)
            acc_sc[g] = alpha * acc_sc[g] + _dot(vb, p.astype(MXU_DTYPE))
            psum = jnp.sum(p.reshape(2 * tk // SUBLANES, SUBLANES, wide), axis=0)
            out.append((m_new, alpha * l + psum))
        return tuple(out)

    init = tuple((jnp.full((1, wide), NEG_BIAS, F32), jnp.zeros((SUBLANES, wide), F32)) for _ in range(N_KV_A))
    fin = lax.fori_loop(0, (nkt + 1) // 2, att_body, init)
    for g, (_, l) in enumerate(fin):
        acc_sc[g] = acc_sc[g] / jnp.sum(l, axis=0, keepdims=True)

    for hp in range(N_HEADS_A // 2):
        g, n = divmod(2 * hp, heads_per_kv)
        pair = jnp.concatenate([acc_sc[g, :, n * tq:(n + 1) * tq],
                                acc_sc[g, :, (n + 1) * tq:(n + 2) * tq]], axis=0)
        out_ref[0, :, hp * LANES:(hp + 1) * LANES] = pair.T.astype(out_ref.dtype)


def _dsa_call(qa, iq, iwt, ikd, kb, vtb):
    b, s, _ = qa.shape
    tq = KEY_TILE
    nq = s // tq
    k_top = min(TOPK_MAX, s // 4)
    tri = jnp.asarray((np.arange(tq)[:, None] >= np.arange(tq)[None, :]).astype(np.float32), MXU_DTYPE)
    qspec = lambda w_: pl.BlockSpec((1, tq, w_), lambda bb, jj: (bb, jj, 0))
    sspec = lambda w_: pl.BlockSpec((1, s, w_), lambda bb, jj: (bb, 0, 0))
    return pl.pallas_call(
        functools.partial(_dsa_kernel, k_top=k_top),
        grid=(b, nq),
        in_specs=[qspec(512), qspec(512),
                  pl.BlockSpec((1, N_IDX_HEADS, tq), lambda bb, jj: (bb, 0, jj)),
                  sspec(128), sspec(KV_W),
                  pl.BlockSpec((1, nq, KV_W, tq), lambda bb, jj: (bb, 0, 0, 0)),
                  pl.BlockSpec((tq, tq), lambda bb, jj: (0, 0))],
        out_specs=qspec(512),
        out_shape=jax.ShapeDtypeStruct((b, s, 512), MXU_DTYPE),
        scratch_shapes=[pltpu.VMEM((nq, tq, tq), I32), pltpu.VMEM((nq, tq, tq), jnp.bfloat16),
                        pltpu.VMEM((nq, tq, tq), F32),
                        pltpu.VMEM((N_IDX_HEADS, tq, LANES), MXU_DTYPE),
                        pltpu.VMEM((N_HEADS_A, tq, LANES), MXU_DTYPE),
                        pltpu.VMEM((N_KV_A, HEAD_DIM_A, (N_HEADS_A // N_KV_A) * tq), F32)],
        compiler_params=pltpu.CompilerParams(
            dimension_semantics=("arbitrary", "arbitrary"), vmem_limit_bytes=VMEM_LIMIT),
        name="dsa_prompt",
    )(qa, iq, iwt, ikd, kb, vtb, tri)


def _ret_kernel(rq_ref, rk_ref, rv_ref, rg_ref, st_ref, dmask_ref, cross_ref, kdec_ref, gdec_ref,
                gng_ref, gnb_ref, out_ref, st_out_ref, st_sc):
    c = pl.program_id(1)

    @pl.when(c == 0)
    def _():
        st_sc[...] = jnp.zeros(st_sc.shape, F32)
        for h in range(N_HEADS_R):
            off = (h % 2) * DK_R
            st_sc[h, off:off + DK_R, :] = st_ref[0, h]

    lane = lax.broadcasted_iota(I32, (rq_ref.shape[1], LANES), 1)
    for h in range(N_HEADS_R):
        sl = slice(h * LANES, (h + 1) * LANES)
        psl = slice((h // 2) * LANES, (h // 2 + 1) * LANES)
        own = (lane < DK_R) if h % 2 == 0 else (lane >= DK_R)
        q = jnp.where(own, rq_ref[0, :, psl], jnp.zeros((), MXU_DTYPE))
        k = jnp.where(own, rk_ref[0, :, psl], jnp.zeros((), MXU_DTYPE))
        v = rv_ref[0, :, sl]
        state = st_sc[h]
        inner = _dot_nt(q, k) * dmask_ref[h]
        o = _dot(inner.astype(MXU_DTYPE), v) + _dot(q, state.astype(MXU_DTYPE)) * cross_ref[:, sl]
        kd = (k.astype(F32) * kdec_ref[:, psl]).astype(MXU_DTYPE)
        st_sc[h] = state * gdec_ref[h] + _dot_tn(kd, v)

        mu = jnp.mean(o, axis=-1, keepdims=True)
        d = o - mu
        var = jnp.mean(d * d, axis=-1, keepdims=True)
        y = d * lax.rsqrt(var + EPS) * gng_ref[:, sl] + gnb_ref[:, sl]
        gate = rg_ref[0, :, sl]
        out_ref[0, :, sl] = (gate * (1.0 / (1.0 + jnp.exp(-gate))) * y).astype(out_ref.dtype)

    @pl.when(c == pl.num_programs(1) - 1)
    def _():
        for h in range(N_HEADS_R):
            off = (h % 2) * DK_R
            st_out_ref[0, h] = st_sc[h, off:off + DK_R, :]


def _ret_tables(cr, c_eff):
    log_g = jnp.log(1.0 - 2.0 ** (-5.0 - jnp.arange(N_HEADS_R, dtype=F32)))
    i = jnp.arange(cr, dtype=F32)
    diff = i[:, None] - i[None, :]
    dmask = jnp.where(diff >= 0, jnp.exp(jnp.maximum(diff, 0.0)[None] * log_g[:, None, None]), 0.0)
    cross = jnp.exp((i + 1.0)[:, None] * log_g[None, :])
    kdec = jnp.where((i < c_eff)[:, None], jnp.exp((c_eff - 1.0 - i)[:, None] * log_g[None, :]), 0.0)
    gdec = jnp.exp(c_eff * log_g)
    return (dmask, jnp.repeat(cross, DV_R, axis=1), jnp.repeat(kdec, DK_R, axis=1),
            jnp.broadcast_to(gdec[:, None, None], (N_HEADS_R, 1, LANES)))


def _ret_call(rq, rk, rv, rg, state, gng, gnb, cr, c_eff):
    b, s, _ = rq.shape
    nc = s // cr
    dmask, cross, kdec, gdec = _ret_tables(cr, c_eff)
    kw = N_HEADS_R * DK_R
    tspec = pl.BlockSpec((1, cr, 512), lambda bb, cc: (bb, cc, 0))
    kspec = pl.BlockSpec((1, cr, kw), lambda bb, cc: (bb, cc, 0))
    sspec = pl.BlockSpec((1, N_HEADS_R, DK_R, DV_R), lambda bb, cc: (bb, 0, 0, 0))
    const = lambda shape: pl.BlockSpec(shape, lambda bb, cc: (0,) * len(shape))
    return pl.pallas_call(
        _ret_kernel,
        grid=(b, nc),
        in_specs=[kspec, kspec, tspec, tspec, sspec,
                  const((N_HEADS_R, cr, cr)), const((cr, 512)), const((cr, kw)),
                  const((N_HEADS_R, 1, LANES)), const((1, 512)), const((1, 512))],
        out_specs=(tspec, sspec),
        out_shape=(jax.ShapeDtypeStruct((b, s, 512), MXU_DTYPE),
                   jax.ShapeDtypeStruct((b, N_HEADS_R, DK_R, DV_R), F32)),
        scratch_shapes=[pltpu.VMEM((N_HEADS_R, LANES, DV_R), F32)],
        compiler_params=pltpu.CompilerParams(dimension_semantics=("arbitrary", "arbitrary")),
        name="retention",
    )(rq, rk, rv, rg, state, dmask, cross, kdec, gdec, gng, gnb)


def _mlp_kernel(x_ref, a_ref, r_ref, wo_ref, wu_ref, wd_ref, g1_ref, g2_ref, g3_ref, out_ref, *, ff_chunk):
    def rms(t, g_ref):
        return t * lax.rsqrt(jnp.mean(t * t, axis=-1, keepdims=True) + EPS) * g_ref[...]

    half = a_ref.shape[1]
    mix = _dot(a_ref[...], wo_ref[:half, :]) + _dot(r_ref[...], wo_ref[half:, :])
    x1 = x_ref[...] + rms(mix, g1_ref)
    h2 = rms(x1, g2_ref).astype(MXU_DTYPE)
    acc = jnp.zeros(x1.shape, F32)
    for c in range(wu_ref.shape[1] // ff_chunk):
        sl = slice(c * ff_chunk, (c + 1) * ff_chunk)
        u = jnp.maximum(_dot(h2, wu_ref[:, sl]), 0.0)
        acc = acc + _dot((u * u).astype(MXU_DTYPE), wd_ref[sl, :])
    out_ref[...] = x1 + rms(acc, g3_ref)


def _mlp_call(x2, a, r, wo, wu, wd, g1, g2, g3, tm):
    n, d = x2.shape
    dff = wu.shape[1]
    tspec = lambda w_: pl.BlockSpec((tm, w_), lambda i: (i, 0))
    const = lambda shape: pl.BlockSpec(shape, lambda i: (0, 0))
    return pl.pallas_call(
        functools.partial(_mlp_kernel, ff_chunk=1024),
        grid=(n // tm,),
        in_specs=[tspec(d), tspec(512), tspec(512), const((d, d)), const((d, dff)), const((dff, d)),
                  const((1, d)), const((1, d)), const((1, d))],
        out_specs=tspec(d),
        out_shape=jax.ShapeDtypeStruct((n, d), F32),
        compiler_params=pltpu.CompilerParams(
            dimension_semantics=("arbitrary",), vmem_limit_bytes=VMEM_LIMIT),
        name="mix_mlp",
    )(x2, a, r, wo, wu, wd, g1, g2, g3)


PAGES_PER_STEP = 32
IDX_ROWS = 8
SELECT_SEQS = 16


def _page_specs(n_feat):
    return [pl.BlockSpec((None, n_feat, PAGE_SIZE),
                         functools.partial(lambda b, i, pt, gg: (pt[b, i * PAGES_PER_STEP + gg], 0, 0), gg=gg))
            for gg in range(PAGES_PER_STEP)]


def _idx_scores_kernel(pt_ref, q_ref, w_ref, knew_ref, *rest, n_pages, t_new):
    pages = rest[:PAGES_PER_STEP]
    keys_ref = rest[PAGES_PER_STEP]
    i = pl.program_id(1)
    q = q_ref[0]
    w = w_ref[0]

    def scores(kpage_t):
        s = jnp.maximum(_dot(q, kpage_t), 0.0) * w
        acc = s[0:IDX_ROWS]
        for h in range(1, N_IDX_HEADS):
            acc = acc + s[h * IDX_ROWS:(h + 1) * IDX_ROWS]
        return _sortable_key(acc)[0]

    for gg in range(PAGES_PER_STEP):
        keys_ref[0, i * PAGES_PER_STEP + gg] = scores(pages[gg][...].astype(MXU_DTYPE))

    @pl.when(i == pl.num_programs(1) - 1)
    def _():
        row = lax.broadcasted_iota(I32, (IDX_ROWS, LANES), 0)
        col = lax.broadcasted_iota(I32, (IDX_ROWS, LANES), 1)
        new_ok = jnp.logical_and(col <= row, col < t_new)
        keys_ref[0, n_pages] = jnp.where(new_ok, scores(knew_ref[0]), INT_MIN)


def _idx_scores_call(page_table, q2, w2, knew_t, kidx_t, t_new):
    db, n_pages = page_table.shape
    n_tiles = n_pages + 1
    bspec = lambda shape: pl.BlockSpec((1,) + shape, lambda b, i, pt: (b,) + (0,) * len(shape))
    return pl.pallas_call(
        functools.partial(_idx_scores_kernel, n_pages=n_pages, t_new=t_new),
        grid_spec=pltpu.PrefetchScalarGridSpec(
            num_scalar_prefetch=1,
            grid=(db, n_pages // PAGES_PER_STEP),
            in_specs=[bspec((N_IDX_HEADS * IDX_ROWS, D_IDX)), bspec((N_IDX_HEADS * IDX_ROWS, LANES)),
                      bspec((D_IDX, PAGE_SIZE))] + _page_specs(D_IDX),
            out_specs=bspec((n_tiles, IDX_ROWS, LANES))),
        out_shape=jax.ShapeDtypeStruct((db, n_tiles, IDX_ROWS, LANES), I32),
        compiler_params=pltpu.CompilerParams(dimension_semantics=("arbitrary", "arbitrary")),
        name="dsa_sample_scores",
    )(page_table, q2, w2, knew_t, *([kidx_t] * PAGES_PER_STEP))


def _idx_select_kernel(keys_ref, tri_ref, bias_ref, *, n_pages, k_top, t_new):
    keys = keys_ref[...]
    n_seq, n_tiles = keys.shape[:2]

    def count_where(pred):
        part = jnp.sum(jnp.where(pred(keys), 1.0, 0.0), axis=1, keepdims=True)
        return jnp.sum(part, axis=-1, keepdims=True)

    t_row = lax.broadcasted_iota(I32, (n_seq, 1, IDX_ROWS, 1), 2)
    n_valid = (jnp.minimum(t_row, t_new - 1) + (n_pages * PAGE_SIZE + 1)).astype(F32)
    thr, _ = _select_threshold(lambda t: count_where(lambda kv: kv >= t), n_valid, float(k_top))
    need = float(k_top) - count_where(lambda kv: kv > thr)

    eq = jnp.where(keys == thr, 1.0, 0.0).astype(MXU_DTYPE)
    pref = _dot(eq.reshape(n_seq * n_tiles * IDX_ROWS, LANES), tri_ref[...]).reshape(keys.shape)
    running = jnp.zeros((n_seq, 1, IDX_ROWS, 1), F32)
    for t in range(n_tiles):
        kv = keys[:, t:t + 1]
        pt = pref[:, t:t + 1]
        take_eq = jnp.where(running + pt <= need, 0.0, NEG_BIAS)
        bias_ref[:, t:t + 1] = jnp.where(kv > thr, 0.0, jnp.where(kv == thr, take_eq, NEG_BIAS))
        running = running + pt[..., LANES - 1:LANES]


def _idx_select_call(keys, n_pages, k_top, t_new):
    db, n_tiles = keys.shape[:2]
    seqs = min(SELECT_SEQS, db)
    tri = jnp.asarray((np.arange(LANES)[:, None] <= np.arange(LANES)[None, :]).astype(np.float32), MXU_DTYPE)
    spec = pl.BlockSpec((seqs, n_tiles, IDX_ROWS, LANES), lambda b: (b, 0, 0, 0))
    return pl.pallas_call(
        functools.partial(_idx_select_kernel, n_pages=n_pages, k_top=k_top, t_new=t_new),
        grid=(db // seqs,),
        in_specs=[spec, pl.BlockSpec((LANES, LANES), lambda b: (0, 0))],
        out_specs=spec,
        out_shape=jax.ShapeDtypeStruct(keys.shape, F32),
        compiler_params=pltpu.CompilerParams(
            dimension_semantics=("arbitrary",), vmem_limit_bytes=VMEM_LIMIT),
        name="dsa_sample_select",
    )(keys, tri)


def _paged_attn_kernel(pt_ref, q_ref, bias_ref, bias_new_ref, knew_ref, vnew_ref, *rest, t_new):
    kpages = rest[:PAGES_PER_STEP]
    vpages = rest[PAGES_PER_STEP:2 * PAGES_PER_STEP]
    out_ref, m_sc, l_sc, acc_sc = rest[2 * PAGES_PER_STEP:]
    i = pl.program_id(1)

    @pl.when(i == 0)
    def _():
        m_sc[...] = jnp.full(m_sc.shape, NEG_BIAS, F32)
        l_sc[...] = jnp.zeros(l_sc.shape, F32)
        acc_sc[...] = jnp.zeros(acc_sc.shape, F32)

    q = q_ref[0]

    def expand(b8):
        return jnp.concatenate(
            [jnp.broadcast_to(b8[t:t + 1, :], (N_HEADS_A, LANES)) for t in range(t_new)], axis=0)

    def step(k_tiles, v_tiles, biases):
        s = jnp.concatenate([_dot(q, kt) * HEAD_DIM_A ** -0.5 + expand(bb)
                             for kt, bb in zip(k_tiles, biases)], axis=1)
        m = m_sc[...]
        m_new = jnp.maximum(m, jnp.max(s, axis=-1, keepdims=True))
        alpha = jnp.exp(m - m_new)
        p = jnp.exp(s - m_new)
        l_sc[...] = alpha * l_sc[...] + jnp.sum(p, axis=-1, keepdims=True)
        pv = _dot_nt(p[:, :LANES].astype(MXU_DTYPE), v_tiles[0])
        for n, vt in enumerate(v_tiles[1:], 1):
            pv = pv + _dot_nt(p[:, n * LANES:(n + 1) * LANES].astype(MXU_DTYPE), vt)
        acc_sc[...] = alpha * acc_sc[...] + pv
        m_sc[...] = m_new

    step([kp[...].astype(MXU_DTYPE) for kp in kpages], [vp[...].astype(MXU_DTYPE) for vp in vpages],
         [bias_ref[0, gg] for gg in range(PAGES_PER_STEP)])

    @pl.when(i == pl.num_programs(1) - 1)
    def _():
        step([knew_ref[0]], [vnew_ref[0]], [bias_new_ref[0, 0]])
        out_ref[0] = acc_sc[...] / l_sc[...]


def _paged_attn_call(page_table, qp, bias, knew_t, vnew_t, k_t, v_t, t_new):
    db, n_pages = page_table.shape
    rows = t_new * N_HEADS_A
    bspec = lambda shape: pl.BlockSpec((1,) + shape, lambda b, i, pt: (b,) + (0,) * len(shape))
    return pl.pallas_call(
        functools.partial(_paged_attn_kernel, t_new=t_new),
        grid_spec=pltpu.PrefetchScalarGridSpec(
            num_scalar_prefetch=1,
            grid=(db, n_pages // PAGES_PER_STEP),
            in_specs=[bspec((rows, KV_W)),
                      pl.BlockSpec((1, PAGES_PER_STEP, IDX_ROWS, LANES), lambda b, i, pt: (b, i, 0, 0)),
                      pl.BlockSpec((1, 1, IDX_ROWS, LANES), lambda b, i, pt: (b, n_pages, 0, 0)),
                      bspec((KV_W, PAGE_SIZE)), bspec((KV_W, PAGE_SIZE))]
                     + _page_specs(KV_W) + _page_specs(KV_W),
            out_specs=bspec((rows, KV_W)),
            scratch_shapes=[pltpu.VMEM((rows, 1), F32), pltpu.VMEM((rows, 1), F32),
                            pltpu.VMEM((rows, KV_W), F32)]),
        out_shape=jax.ShapeDtypeStruct((db, rows, KV_W), F32),
        compiler_params=pltpu.CompilerParams(dimension_semantics=("arbitrary", "arbitrary")),
        name="dsa_sample_attend",
    )(page_table, qp, bias, bias, knew_t, vnew_t, *([k_t] * PAGES_PER_STEP), *([v_t] * PAGES_PER_STEP))


def _layout_w_in(w_in):
    cuts = np.cumsum(PROJ_SIZES)[:-1].tolist()
    wq, wk, wv, wiq, wik, wiw, wrq, wrk, wrv, wrg = jnp.split(w_in, cuts, axis=1)
    ikw = jnp.pad(jnp.concatenate([wik, wiw], axis=1), ((0, 0), (0, LANES - D_IDX - N_IDX_HEADS)))
    w = jnp.concatenate([wq, wk, wv, wiq, ikw, wrq, wrk, wrv, wrg], axis=1)
    assert w.shape[1] == W_PROJ
    return w.astype(MXU_DTYPE)


def _rope_tables(pos):
    posf = pos.astype(F32)[:, None]

    def tables(rot_dim, theta, period, width):
        half = rot_dim // 2
        inv = theta ** (-jnp.arange(half, dtype=F32) / half)
        ang = posf * inv[None, :]
        cos, sin = jnp.cos(ang), jnp.sin(ang)
        n = pos.shape[0]
        rest = period - rot_dim
        c = jnp.concatenate([cos, cos, jnp.ones((n, rest), F32)], axis=1)
        su = jnp.concatenate([-sin, jnp.zeros((n, half + rest), F32)], axis=1)
        sd = jnp.concatenate([jnp.zeros((n, half), F32), sin, jnp.zeros((n, rest), F32)], axis=1)
        return [jnp.tile(t, (1, width // period)) for t in (c, su, sd)]

    a = tables(ROT_DIM_A, ROPE_THETA, HEAD_DIM_A, LANES)
    r = tables(DK_R, RET_THETA, DK_R, LANES)
    return jnp.concatenate(a + r, axis=1)


def kernel(x_prompt, x_sample, cache_k, cache_v, cache_kidx, state_ret, page_table, w_in, w_out, w_up, w_down,
           g_pre_mix, g_post_mix, g_pre_mlp, g_post_mlp, kidx_ln_g, kidx_ln_b, ret_gn_g, ret_gn_b):
    b, s, d = x_prompt.shape
    db, t_new, _ = x_sample.shape
    n_pages = page_table.shape[1]
    past = n_pages * PAGE_SIZE
    layer = 0
    assert w_in.shape[0] == 1 and t_new <= IDX_ROWS and n_pages % PAGES_PER_STEP == 0
    assert s % (2 * KEY_TILE) == 0 and (db * SAMPLE_ROWS) % KEY_TILE == 0

    w_proj = _layout_w_in(w_in[layer])
    wo, wu, wd = (w_[layer].astype(MXU_DTYPE) for w_ in (w_out, w_up, w_down))
    row = lambda v: v[layer][None, :].astype(F32)
    g_pre, g1, g2, g3 = row(g_pre_mix), row(g_post_mix), row(g_pre_mlp), row(g_post_mlp)
    lng = jnp.pad(row(kidx_ln_g), ((0, 0), (0, LANES - D_IDX)))
    lnb = jnp.pad(row(kidx_ln_b), ((0, 0), (0, LANES - D_IDX)))
    gng, gnb = row(ret_gn_g), row(ret_gn_b)

    tm = 2 * KEY_TILE
    pp = _proj_call(x_prompt, _rope_tables(jnp.arange(s)), g_pre, w_proj, lng, lnb, tm)
    a_out = _dsa_call(pp["qa"], pp["iq"], pp["iwT"], pp["ikd"], pp["kb"], pp["vTb"])
    zero_state = jnp.zeros((b, N_HEADS_R, DK_R, DV_R), F32)
    r_out, ret_prompt = _ret_call(pp["rq"], pp["rk"], pp["rv"], pp["rg"], zero_state, gng, gnb,
                                  RET_CHUNK, float(RET_CHUNK))
    y_prompt = _mlp_call(x_prompt.reshape(b * s, d), a_out.reshape(b * s, 512), r_out.reshape(b * s, 512),
                         wo, wu, wd, g1, g2, g3, tm).reshape(b, s, d)
    kv_out = lambda t: t.reshape(b, N_KV_A, HEAD_DIM_A, s).transpose(0, 3, 1, 2)[None]
    k_prompt, v_prompt = kv_out(pp["kT"]), kv_out(pp["vT"])
    kidx_prompt = pp["ikT"].transpose(0, 2, 1)[None]

    sr = SAMPLE_ROWS
    n_s = db * sr
    xs = jnp.pad(x_sample, ((0, 0), (0, sr - t_new), (0, 0))).reshape(1, n_s, d)
    pos_s = jnp.tile(past + jnp.arange(sr), db)
    tm_s = min(2 * KEY_TILE, n_s)
    ps = _proj_call(xs, _rope_tables(pos_s), g_pre, w_proj, lng, lnb, tm_s)
    per_seq = lambda t: t.reshape(db, sr, t.shape[-1])
    per_seq_t = lambda t: t.reshape(t.shape[1], db, sr).transpose(1, 0, 2)
    k_s, v_s, ik_s, iw_s = (per_seq_t(ps[n_]) for n_ in ("kT", "vT", "ikT", "iwT"))

    k_top = min(TOPK_MAX, (past + t_new) // 4)
    iq_s = per_seq(ps["iq"])[:, :IDX_ROWS].reshape(db, IDX_ROWS, N_IDX_HEADS, D_IDX)
    q2 = iq_s.transpose(0, 2, 1, 3).reshape(db, N_IDX_HEADS * IDX_ROWS, D_IDX)
    w2 = jnp.broadcast_to(iw_s[:, :, :IDX_ROWS].reshape(db, N_IDX_HEADS * IDX_ROWS, 1),
                          (db, N_IDX_HEADS * IDX_ROWS, LANES))
    pad_slots = lambda t: jnp.pad(t, ((0, 0), (0, 0), (0, PAGE_SIZE - sr))).astype(MXU_DTYPE)
    kidx_t = cache_kidx[layer].transpose(0, 2, 1)
    k_t = cache_k[layer].transpose(0, 2, 3, 1).reshape(-1, KV_W, PAGE_SIZE)
    v_t = cache_v[layer].transpose(0, 2, 3, 1).reshape(-1, KV_W, PAGE_SIZE)
    keys = _idx_scores_call(page_table, q2, w2, pad_slots(ik_s), kidx_t, t_new)
    bias = _idx_select_call(keys, n_pages, k_top, t_new)

    qa_s = per_seq(ps["qa"])[:, :t_new].reshape(db, t_new, N_HEADS_A, HEAD_DIM_A)
    grp = (jnp.arange(N_HEADS_A) // (N_HEADS_A // N_KV_A))[None, None, :, None, None]
    own = grp == jnp.arange(N_KV_A)[None, None, None, :, None]
    qp = jnp.where(own, qa_s[:, :, :, None, :], jnp.zeros((), MXU_DTYPE)).reshape(db, t_new * N_HEADS_A, KV_W)
    o_s = _paged_attn_call(page_table, qp, bias, pad_slots(k_s), pad_slots(v_s), k_t, v_t, t_new)
    o_s = o_s.reshape(db, t_new, N_HEADS_A, N_KV_A, HEAD_DIM_A)
    a_s = jnp.where(own, o_s, 0.0).sum(axis=3)
    a_s = jnp.pad(a_s.reshape(db, t_new, 512), ((0, 0), (0, sr - t_new), (0, 0))).astype(MXU_DTYPE)

    r_s, ret_sample = _ret_call(per_seq(ps["rq"]), per_seq(ps["rk"]), per_seq(ps["rv"]), per_seq(ps["rg"]),
                                state_ret[layer].astype(F32), gng, gnb, sr, float(t_new))
    y_s = _mlp_call(xs.reshape(n_s, d), a_s.reshape(n_s, 512), r_s.reshape(n_s, 512),
                    wo, wu, wd, g1, g2, g3, tm_s).reshape(db, sr, d)[:, :t_new]

    kv_s_out = lambda t: t[:, :, :t_new].reshape(db, N_KV_A, HEAD_DIM_A, t_new).transpose(0, 3, 1, 2)[None]
    return (y_prompt, y_s, k_prompt, v_prompt, kidx_prompt, ret_prompt[None],
            kv_s_out(k_s), kv_s_out(v_s), ik_s[:, :, :t_new].transpose(0, 2, 1)[None], ret_sample[None])
```

```python
import functools

import numpy as np
import jax
import jax.numpy as jnp
from jax import lax
from jax.experimental import pallas as pl
from jax.experimental.pallas import tpu as pltpu

F32 = jnp.float32
I32 = jnp.int32
MXU_DTYPE = jnp.bfloat16

LANES = 128
SUBLANES = 8
N_HEADS_A = 8
N_KV_A = 2
HEAD_DIM_A = 64
ROT_DIM_A = 16
ROPE_THETA = 500000.0
N_IDX_HEADS = 8
D_IDX = 64
TOPK_MAX = 256
N_HEADS_R = 4
DV_R = 128
DK_R = 64
RET_THETA = 10000.0
RET_CHUNK = 128
PAGE_SIZE = 128
EPS = 1e-6
PROJ_SIZES = (N_HEADS_A * HEAD_DIM_A, N_KV_A * HEAD_DIM_A, N_KV_A * HEAD_DIM_A,
              N_IDX_HEADS * D_IDX, D_IDX, N_IDX_HEADS,
              N_HEADS_R * DK_R, N_HEADS_R * DK_R, N_HEADS_R * DV_R, N_HEADS_R * DV_R)
KV_W = N_KV_A * HEAD_DIM_A

INT_MIN = -(2 ** 31)
NEG_BIAS = -1e30
SAMPLE_ROWS = 16
KEY_TILE = 256
RET_SEQS = 4
SEARCH_BITS_PER_CHECK = 4
VMEM_LIMIT = 56 * 1024 * 1024

_SEG = {}
_off = 0
for _name, _w in (("q", 512), ("k", 128), ("v", 128), ("iq", 512),
                  ("ikw", 128), ("rq", 256), ("rk", 256), ("rv", 512), ("rg", 512)):
    _SEG[_name] = (_off, _off + _w)
    _off += _w
W_PROJ = _off


def _dot(a, b):
    return jnp.dot(a, b, preferred_element_type=F32)


def _dot_nt(a, b):
    return lax.dot_general(a, b, (((1,), (1,)), ((), ())), preferred_element_type=F32)


def _dot_tn(a, b):
    return lax.dot_general(a, b, (((0,), (0,)), ((), ())), preferred_element_type=F32)


def _tile_lanes(t, width):
    reps = width // t.shape[1]
    return t if reps == 1 else jnp.concatenate([t] * reps, axis=1)


def _tree_sum(parts):
    while len(parts) > 1:
        parts = [a + b for a, b in zip(parts[::2], parts[1::2])]
    return parts[0]


def _sortable_key(score):
    score = jnp.where(score == 0.0, 0.0, score)
    bits = pltpu.bitcast(score, I32)
    key = jnp.where(bits >= 0, bits, bits ^ 0x7FFFFFFF)
    return key, pltpu.bitcast(bits & -65536, F32)


def _coarse_threshold(t):
    t16 = lax.shift_right_arithmetic(t, 16)
    t16 = jnp.where(jnp.logical_and(t16 > 0, t16 < 0x80), 0x80, t16)
    b16 = jnp.where(t16 >= 0, t16, t16 ^ 0x7FFF)
    return pltpu.bitcast(lax.shift_left(b16, 16), F32).astype(jnp.bfloat16)


def _select_threshold(count_ge, n_valid, k_top, count_ge_coarse=None):
    def refine(count_fn, state, last_bit):
        def cond(c):
            bit, _, cnt = c
            return jnp.logical_and(bit >= last_bit, jnp.max(cnt) > k_top)

        def step(_, c):
            bit, u, cnt = c
            cand = u | lax.shift_left(jnp.int32(1), bit)
            c_new = count_fn(cand ^ INT_MIN)
            ok = c_new >= k_top
            return bit - 1, jnp.where(ok, cand, u), jnp.where(ok, c_new, cnt)

        return lax.while_loop(cond, lambda c: lax.fori_loop(0, SEARCH_BITS_PER_CHECK, step, c), state)

    state = (jnp.int32(31), jnp.zeros(n_valid.shape, I32), n_valid)
    if count_ge_coarse is not None:
        state = refine(count_ge_coarse, state, 16)
    _, u, cnt = refine(count_ge, state, 0)
    return jnp.maximum(u ^ INT_MIN, INT_MIN + 1), cnt


def _rope(z, cos_t, sin_up_t, sin_dn_t, half):
    w = z.shape[1]
    up = pltpu.roll(z, w - half, 1)
    dn = pltpu.roll(z, half, 1)
    return z * _tile_lanes(cos_t, w) + up * _tile_lanes(sin_up_t, w) + dn * _tile_lanes(sin_dn_t, w)


def _proj_kernel(x_ref, tab_ref, g_ref, w_ref, lng_ref, lnb_ref,
                 qa_o, kb_o, iq_o, ikd_o, rq_o, rk_o, rv_o, rg_o, kt_o, vt_o, vtb_o, ikt_o, iwt_o):
    half_a = ROT_DIM_A // 2
    half_r = DK_R // 2
    for r in range(x_ref.shape[1] // KEY_TILE):
        rows = slice(r * KEY_TILE, (r + 1) * KEY_TILE)
        x = x_ref[0, rows, :]
        h = (x * lax.rsqrt(jnp.mean(x * x, axis=-1, keepdims=True) + EPS) * g_ref[...]).astype(MXU_DTYPE)

        def seg(name, h=h):
            lo, hi = _SEG[name]
            return _dot(h, w_ref[:, lo:hi])

        tab = tab_ref[rows, :]
        c_a, su_a, sd_a, c_r, su_r, sd_r = [tab[:, i * LANES:(i + 1) * LANES] for i in range(6)]

        qa_o[0, rows, :] = _rope(seg("q"), c_a, su_a, sd_a, half_a).astype(qa_o.dtype)
        iq_o[0, rows, :] = _rope(seg("iq"), c_a, su_a, sd_a, half_a).astype(iq_o.dtype)
        rq_o[0, rows, :] = _rope(seg("rq"), c_r, su_r, sd_r, half_r).astype(rq_o.dtype)
        rk_o[0, rows, :] = _rope(seg("rk") * DK_R ** -0.5, c_r, su_r, sd_r, half_r).astype(rk_o.dtype)
        rv_o[0, rows, :] = seg("rv").astype(rv_o.dtype)
        rg_o[0, rows, :] = seg("rg")

        k = _rope(seg("k"), c_a, su_a, sd_a, half_a)
        kb_o[0, rows, :] = k.astype(kb_o.dtype)
        kt_o[0, :, rows] = k.T
        vt = seg("v").T
        vt_o[0, :, rows] = vt
        vtb_o[0, r] = vt.astype(vtb_o.dtype)

        zi = seg("ikw")
        lane = lax.broadcasted_iota(I32, zi.shape, 1)
        is_ik = lane < D_IDX
        mu = jnp.sum(jnp.where(is_ik, zi, 0.0), axis=-1, keepdims=True) * (1.0 / D_IDX)
        d = jnp.where(is_ik, zi - mu, 0.0)
        var = jnp.sum(d * d, axis=-1, keepdims=True) * (1.0 / D_IDX)
        y = d * lax.rsqrt(var + EPS) * lng_ref[...] + lnb_ref[...]
        ikr = jnp.where(is_ik, _rope(y, c_a, su_a, sd_a, half_a), 0.0)
        ikd_o[0, rows, :] = (ikr + pltpu.roll(ikr, D_IDX, 1)).astype(ikd_o.dtype)
        both_t = jnp.where(is_ik, ikr, zi * (N_IDX_HEADS * D_IDX) ** -0.5).T
        ikt_o[0, :, rows] = both_t[:D_IDX]
        iwt_o[0, :, rows] = both_t[D_IDX:D_IDX + N_IDX_HEADS]


def _proj_call(x3, tab, g, w, lng, lnb, tm):
    bx, n, d = x3.shape
    grid = (n // tm, bx)
    tspec = lambda w_: pl.BlockSpec((1, tm, w_), lambda i, b: (b, i, 0))
    ttspec = lambda r_: pl.BlockSpec((1, r_, tm), lambda i, b: (b, 0, i))
    const = lambda shape: pl.BlockSpec(shape, lambda i, b: (0,) * len(shape))
    nat = (("qa", 512, MXU_DTYPE), ("kb", KV_W, MXU_DTYPE), ("iq", 512, MXU_DTYPE), ("ikd", 128, MXU_DTYPE),
           ("rq", 256, MXU_DTYPE), ("rk", 256, MXU_DTYPE), ("rv", 512, MXU_DTYPE), ("rg", 512, F32))
    tr = (("kT", KV_W, F32), ("vT", KV_W, F32))
    tr2 = (("ikT", D_IDX, F32), ("iwT", N_IDX_HEADS, F32))
    out_specs = ([tspec(w_) for _, w_, _ in nat] + [ttspec(r_) for _, r_, _ in tr]
                 + [pl.BlockSpec((1, tm // KEY_TILE, KV_W, KEY_TILE), lambda i, b: (b, i, 0, 0))]
                 + [ttspec(r_) for _, r_, _ in tr2])
    out_shape = ([jax.ShapeDtypeStruct((bx, n, w_), dt) for _, w_, dt in nat]
                 + [jax.ShapeDtypeStruct((bx, r_, n), dt) for _, r_, dt in tr]
                 + [jax.ShapeDtypeStruct((bx, n // KEY_TILE, KV_W, KEY_TILE), MXU_DTYPE)]
                 + [jax.ShapeDtypeStruct((bx, r_, n), dt) for _, r_, dt in tr2])
    names = [o[0] for o in nat] + [o[0] for o in tr] + ["vTb"] + [o[0] for o in tr2]
    res = pl.pallas_call(
        _proj_kernel,
        grid=grid,
        in_specs=[tspec(d),
                  pl.BlockSpec((tm, 6 * LANES), lambda i, b: (i, 0)),
                  const((1, d)), const((d, W_PROJ)), const((1, LANES)), const((1, LANES))],
        out_specs=tuple(out_specs),
        out_shape=tuple(out_shape),
        compiler_params=pltpu.CompilerParams(
            dimension_semantics=("arbitrary", "arbitrary"), vmem_limit_bytes=VMEM_LIMIT),
        name="proj",
    )(x3, tab, g, w, lng, lnb)
    return dict(zip(names, res))


def _dsa_kernel(qa_ref, iq_ref, iwt_ref, iqn_ref, iwtn_ref, ikd_ref, kb_ref, vtb_ref, tri_ref, out_ref,
                keys_sc, top_sc, bias_sc, iqp_sc, qap_sc, acc_sc, *, k_top):
    tq = tk = KEY_TILE
    j = pl.program_id(1)
    q0 = j * tq
    nkt = j + 1
    heads_per_kv = N_HEADS_A // N_KV_A

    lane = lax.broadcasted_iota(I32, (tq, LANES), 1)
    lo_half = lane < HEAD_DIM_A
    hi_half = jnp.logical_not(lo_half)

    def load_indexer_queries(src_ref):
        for h in range(N_IDX_HEADS):
            sl = slice((h // 2) * LANES, (h // 2 + 1) * LANES)
            iqp_sc[h] = jnp.where(lo_half if h % 2 == 0 else hi_half, src_ref[0, :, sl], jnp.zeros((), MXU_DTYPE))

    for h in range(N_HEADS_A):
        sl = slice((h // 2) * LANES, (h // 2 + 1) * LANES)
        g = h // heads_per_kv
        q = qa_ref[0, :, sl].astype(F32) * HEAD_DIM_A ** -0.5
        if h % 2 != g:
            q = pltpu.roll(q, HEAD_DIM_A, 1)
        qap_sc[h] = jnp.where(lo_half if g == 0 else hi_half, q, 0.0).astype(MXU_DTYPE)

    kpos = lax.broadcasted_iota(I32, (tk, tq), 0)
    qcol = lax.broadcasted_iota(I32, (tk, tq), 1)

    def score_tile(kt, qstart, w_ref):
        k0 = pl.multiple_of(kt * tk, tk)
        kb = ikd_ref[0, pl.ds(k0, tk), :]
        acc = jnp.zeros((tk, tq), F32)
        for h in range(N_IDX_HEADS):
            acc = acc + jnp.maximum(_dot_nt(kb, iqp_sc[h]), 0.0) * w_ref[0, h:h + 1, :]
        key, top = _sortable_key(acc)
        causal = kpos + k0 <= qcol + qstart
        keys_sc[kt] = jnp.where(causal, key, INT_MIN)
        top_sc[kt] = jnp.where(causal, top, jnp.nan).astype(top_sc.dtype)

    @pl.when(j == 0)
    def _():
        load_indexer_queries(iq_ref)
        score_tile(0, 0, iwt_ref)

    def count_where(pred):
        def body(kt, part):
            hit = jnp.where(pred(keys_sc[kt]), 1.0, 0.0)
            return part + _tree_sum([hit[r * SUBLANES:(r + 1) * SUBLANES] for r in range(tk // SUBLANES)])
        part = lax.fori_loop(0, nkt, body, jnp.zeros((SUBLANES, tq), F32))
        return jnp.sum(part, axis=0, keepdims=True)

    def count_coarse(t):
        tb = _coarse_threshold(t)
        rows = 2 * SUBLANES

        def body(kt, part):
            hit = jnp.where(top_sc[kt] >= tb, jnp.ones((), top_sc.dtype), jnp.zeros((), top_sc.dtype))
            return part + _tree_sum([hit[r * rows:(r + 1) * rows] for r in range(tk // rows)]).astype(F32)
        part = lax.fori_loop(0, nkt, body, jnp.zeros((rows, tq), F32))
        return jnp.sum(part, axis=0, keepdims=True)

    n_valid = (lax.broadcasted_iota(I32, (1, tq), 1) + (q0 + 1)).astype(F32)
    thr, n_ge = _select_threshold(lambda t: count_where(lambda kv: kv >= t), n_valid, float(k_top),
                                  count_ge_coarse=count_coarse)
    has_ties = jnp.max(n_ge) > float(k_top)

    @pl.when(jnp.logical_not(has_ties))
    def _():
        def bias_body(kt, carry):
            bias_sc[kt] = jnp.where(keys_sc[kt] >= thr, 0.0, NEG_BIAS)
            return carry
        lax.fori_loop(0, nkt, bias_body, 0)

    @pl.when(has_ties)
    def _():
        need = float(k_top) - count_where(lambda kv: kv > thr)

        def bias_body(kt, running):
            kv = keys_sc[kt]
            eq = kv == thr
            pref = _dot(tri_ref[...], jnp.where(eq, 1.0, 0.0).astype(MXU_DTYPE))
            take_eq = jnp.where(running + pref <= need, 0.0, NEG_BIAS)
            bias_sc[kt] = jnp.where(kv > thr, 0.0, jnp.where(eq, take_eq, NEG_BIAS))
            return running + pref[tk - 1:tk, :]

        lax.fori_loop(0, nkt, bias_body, jnp.zeros((1, tq), F32))

    wide = heads_per_kv * tq
    acc_sc[...] = jnp.zeros(acc_sc.shape, F32)

    @pl.when(nkt % 2 == 1)
    def _():
        bias_sc[nkt] = jnp.full((tk, tq), NEG_BIAS, F32)

    load_indexer_queries(iqn_ref)
    q0_next = q0 + tq

    def att_body(kp, carry):
        k0 = pl.multiple_of(kp * (2 * tk), 2 * tk)
        bias = _tile_lanes(bias_sc[pl.ds(2 * kp, 2)].reshape(2 * tk, tq), wide)
        kb = kb_ref[0, pl.ds(k0, 2 * tk), :]
        out = []
        for g, (m, l) in enumerate(carry):
            qg = qap_sc[g * heads_per_kv:(g + 1) * heads_per_kv].reshape(wide, LANES)
            gsl = slice(g * HEAD_DIM_A, (g + 1) * HEAD_DIM_A)
            vb = jnp.concatenate([vtb_ref[0, 2 * kp, gsl, :], vtb_ref[0, 2 * kp + 1, gsl, :]], axis=1)
            s = _dot_nt(kb, qg) + bias
            m_new = jnp.maximum(m, jnp.max(s, axis=0, keepdims=True))
            alpha = jnp.exp(m - m_new)
            p = jnp.exp(s - m_new)
            acc_sc[g] = alpha * acc_sc[g] + _dot(vb, p.astype(MXU_DTYPE))
            psum = jnp.sum(p.reshape(2 * tk // SUBLANES, SUBLANES, wide), axis=0)
            out.append((m_new, alpha * l + psum))
            score_tile(2 * kp + g, q0_next, iwtn_ref)
        return tuple(out)

    init = tuple((jnp.full((1, wide), NEG_BIAS, F32), jnp.zeros((SUBLANES, wide), F32)) for _ in range(N_KV_A))
    fin = lax.fori_loop(0, (nkt + 1) // 2, att_body, init)
    for g, (_, l) in enumerate(fin):
        acc_sc[g] = acc_sc[g] / jnp.sum(l, axis=0, keepdims=True)

    @pl.when(jnp.logical_and(nkt % 2 == 0, j + 1 < pl.num_programs(1)))
    def _():
        score_tile(nkt, q0_next, iwtn_ref)

    for hp in range(N_HEADS_A // 2):
        g, n = divmod(2 * hp, heads_per_kv)
        pair = jnp.concatenate([acc_sc[g, :, n * tq:(n + 1) * tq],
                                acc_sc[g, :, (n + 1) * tq:(n + 2) * tq]], axis=0)
        out_ref[0, :, hp * LANES:(hp + 1) * LANES] = pair.T.astype(out_ref.dtype)


def _dsa_call(qa, iq, iwt, ikd, kb, vtb):
    b, s, _ = qa.shape
    tq = KEY_TILE
    nq = s // tq
    k_top = min(TOPK_MAX, s // 4)
    tri = jnp.asarray((np.arange(tq)[:, None] >= np.arange(tq)[None, :]).astype(np.float32), MXU_DTYPE)
    qspec = lambda w_: pl.BlockSpec((1, tq, w_), lambda bb, jj: (bb, jj, 0))
    sspec = lambda w_: pl.BlockSpec((1, s, w_), lambda bb, jj: (bb, 0, 0))
    return pl.pallas_call(
        functools.partial(_dsa_kernel, k_top=k_top),
        grid=(b, nq),
        in_specs=[qspec(512), qspec(512),
                  pl.BlockSpec((1, N_IDX_HEADS, tq), lambda bb, jj: (bb, 0, jj)),
                  pl.BlockSpec((1, tq, 512), lambda bb, jj: (bb, jnp.minimum(jj + 1, nq - 1), 0)),
                  pl.BlockSpec((1, N_IDX_HEADS, tq), lambda bb, jj: (bb, 0, jnp.minimum(jj + 1, nq - 1))),
                  sspec(128), sspec(KV_W),
                  pl.BlockSpec((1, nq, KV_W, tq), lambda bb, jj: (bb, 0, 0, 0)),
                  pl.BlockSpec((tq, tq), lambda bb, jj: (0, 0))],
        out_specs=qspec(512),
        out_shape=jax.ShapeDtypeStruct((b, s, 512), MXU_DTYPE),
        scratch_shapes=[pltpu.VMEM((nq, tq, tq), I32), pltpu.VMEM((nq, tq, tq), jnp.bfloat16),
                        pltpu.VMEM((nq, tq, tq), F32),
                        pltpu.VMEM((N_IDX_HEADS, tq, LANES), MXU_DTYPE),
                        pltpu.VMEM((N_HEADS_A, tq, LANES), MXU_DTYPE),
                        pltpu.VMEM((N_KV_A, HEAD_DIM_A, (N_HEADS_A // N_KV_A) * tq), F32)],
        compiler_params=pltpu.CompilerParams(
            dimension_semantics=("arbitrary", "arbitrary"), vmem_limit_bytes=VMEM_LIMIT),
        name="dsa_prompt",
    )(qa, iq, iwt, iq, iwt, ikd, kb, vtb, tri)


def _ret_kernel(rq_ref, rk_ref, rv_ref, rg_ref, st_ref, dmask_ref, cross_ref, kdec_ref, gdec_ref,
                gng_ref, gnb_ref, out_ref, st_out_ref, st_sc):
    c = pl.program_id(1)
    n_seq = rq_ref.shape[0]

    @pl.when(c == 0)
    def _():
        st_sc[...] = jnp.zeros(st_sc.shape, F32)
        for i in range(n_seq):
            for h in range(N_HEADS_R):
                off = (h % 2) * DK_R
                st_sc[i, h, off:off + DK_R, :] = st_ref[i, h]

    lane = lax.broadcasted_iota(I32, (rq_ref.shape[1], LANES), 1)
    for i in range(n_seq):
        for h in range(N_HEADS_R):
            sl = slice(h * LANES, (h + 1) * LANES)
            psl = slice((h // 2) * LANES, (h // 2 + 1) * LANES)
            own = (lane < DK_R) if h % 2 == 0 else (lane >= DK_R)
            q = jnp.where(own, rq_ref[i, :, psl], jnp.zeros((), MXU_DTYPE))
            k = jnp.where(own, rk_ref[i, :, psl], jnp.zeros((), MXU_DTYPE))
            v = rv_ref[i, :, sl]
            state = st_sc[i, h]
            inner = _dot_nt(q, k) * dmask_ref[h]
            o = _dot(inner.astype(MXU_DTYPE), v) + _dot(q, state.astype(MXU_DTYPE)) * cross_ref[:, sl]
            kd = (k.astype(F32) * kdec_ref[:, psl]).astype(MXU_DTYPE)
            st_sc[i, h] = state * gdec_ref[h] + _dot_tn(kd, v)

            mu = jnp.mean(o, axis=-1, keepdims=True)
            d = o - mu
            var = jnp.mean(d * d, axis=-1, keepdims=True)
            y = d * lax.rsqrt(var + EPS) * gng_ref[:, sl] + gnb_ref[:, sl]
            gate = rg_ref[i, :, sl]
            out_ref[i, :, sl] = (gate * (1.0 / (1.0 + jnp.exp(-gate))) * y).astype(out_ref.dtype)

    @pl.when(c == pl.num_programs(1) - 1)
    def _():
        for i in range(n_seq):
            for h in range(N_HEADS_R):
                off = (h % 2) * DK_R
                st_out_ref[i, h] = st_sc[i, h, off:off + DK_R, :]


def _ret_tables(cr, c_eff):
    log_g = jnp.log(1.0 - 2.0 ** (-5.0 - jnp.arange(N_HEADS_R, dtype=F32)))
    i = jnp.arange(cr, dtype=F32)
    diff = i[:, None] - i[None, :]
    dmask = jnp.where(diff >= 0, jnp.exp(jnp.maximum(diff, 0.0)[None] * log_g[:, None, None]), 0.0)
    cross = jnp.exp((i + 1.0)[:, None] * log_g[None, :])
    kdec = jnp.where((i < c_eff)[:, None], jnp.exp((c_eff - 1.0 - i)[:, None] * log_g[None, :]), 0.0)
    gdec = jnp.exp(c_eff * log_g)
    return (dmask, jnp.repeat(cross, DV_R, axis=1), jnp.repeat(kdec, DK_R, axis=1),
            jnp.broadcast_to(gdec[:, None, None], (N_HEADS_R, 1, LANES)))


def _ret_call(rq, rk, rv, rg, state, gng, gnb, cr, c_eff):
    b, s, _ = rq.shape
    nc = s // cr
    dmask, cross, kdec, gdec = _ret_tables(cr, c_eff)
    kw = N_HEADS_R * DK_R
    ns = max(n for n in (RET_SEQS, 2, 1) if b % n == 0)
    tspec = pl.BlockSpec((ns, cr, 512), lambda bb, cc: (bb, cc, 0))
    kspec = pl.BlockSpec((ns, cr, kw), lambda bb, cc: (bb, cc, 0))
    sspec = pl.BlockSpec((ns, N_HEADS_R, DK_R, DV_R), lambda bb, cc: (bb, 0, 0, 0))
    const = lambda shape: pl.BlockSpec(shape, lambda bb, cc: (0,) * len(shape))
    return pl.pallas_call(
        _ret_kernel,
        grid=(b // ns, nc),
        in_specs=[kspec, kspec, tspec, tspec, sspec,
                  const((N_HEADS_R, cr, cr)), const((cr, 512)), const((cr, kw)),
                  const((N_HEADS_R, 1, LANES)), const((1, 512)), const((1, 512))],
        out_specs=(tspec, sspec),
        out_shape=(jax.ShapeDtypeStruct((b, s, 512), MXU_DTYPE),
                   jax.ShapeDtypeStruct((b, N_HEADS_R, DK_R, DV_R), F32)),
        scratch_shapes=[pltpu.VMEM((ns, N_HEADS_R, LANES, DV_R), F32)],
        compiler_params=pltpu.CompilerParams(dimension_semantics=("arbitrary", "arbitrary")),
        name="retention",
    )(rq, rk, rv, rg, state, dmask, cross, kdec, gdec, gng, gnb)


def _mlp_kernel(x_ref, a_ref, r_ref, wo_ref, wu_ref, wd_ref, g1_ref, g2_ref, g3_ref, out_ref, *, ff_chunk):
    def rms(t, g_ref):
        return t * lax.rsqrt(jnp.mean(t * t, axis=-1, keepdims=True) + EPS) * g_ref[...]

    half = a_ref.shape[1]
    mix = _dot(a_ref[...], wo_ref[:half, :]) + _dot(r_ref[...], wo_ref[half:, :])
    x1 = x_ref[...] + rms(mix, g1_ref)
    h2 = rms(x1, g2_ref).astype(MXU_DTYPE)
    acc = jnp.zeros(x1.shape, F32)
    for c in range(wu_ref.shape[1] // ff_chunk):
        sl = slice(c * ff_chunk, (c + 1) * ff_chunk)
        u = jnp.maximum(_dot(h2, wu_ref[:, sl]), 0.0)
        acc = acc + _dot((u * u).astype(MXU_DTYPE), wd_ref[sl, :])
    out_ref[...] = x1 + rms(acc, g3_ref)


def _mlp_call(x2, a, r, wo, wu, wd, g1, g2, g3, tm):
    n, d = x2.shape
    dff = wu.shape[1]
    tspec = lambda w_: pl.BlockSpec((tm, w_), lambda i: (i, 0))
    const = lambda shape: pl.BlockSpec(shape, lambda i: (0, 0))
    return pl.pallas_call(
        functools.partial(_mlp_kernel, ff_chunk=1024),
        grid=(n // tm,),
        in_specs=[tspec(d), tspec(512), tspec(512), const((d, d)), const((d, dff)), const((dff, d)),
                  const((1, d)), const((1, d)), const((1, d))],
        out_specs=tspec(d),
        out_shape=jax.ShapeDtypeStruct((n, d), F32),
        compiler_params=pltpu.CompilerParams(
            dimension_semantics=("arbitrary",), vmem_limit_bytes=VMEM_LIMIT),
        name="mix_mlp",
    )(x2, a, r, wo, wu, wd, g1, g2, g3)


PAGES_PER_STEP = 32
IDX_ROWS = 8
SELECT_SEQS = 16


def _page_specs(n_feat):
    return [pl.BlockSpec((None, n_feat, PAGE_SIZE),
                         functools.partial(lambda b, i, pt, gg: (pt[b, i * PAGES_PER_STEP + gg], 0, 0), gg=gg))
            for gg in range(PAGES_PER_STEP)]


def _idx_scores_kernel(pt_ref, q_ref, w_ref, knew_ref, *rest, n_pages, t_new):
    pages = rest[:PAGES_PER_STEP]
    keys_ref = rest[PAGES_PER_STEP]
    i = pl.program_id(1)
    q = q_ref[0]
    w = w_ref[0]

    def scores(kpage_t):
        s = jnp.maximum(_dot(q, kpage_t), 0.0) * w
        acc = s[0:IDX_ROWS]
        for h in range(1, N_IDX_HEADS):
            acc = acc + s[h * IDX_ROWS:(h + 1) * IDX_ROWS]
        return _sortable_key(acc)[0]

    for gg in range(PAGES_PER_STEP):
        keys_ref[0, i * PAGES_PER_STEP + gg] = scores(pages[gg][...].astype(MXU_DTYPE))

    @pl.when(i == pl.num_programs(1) - 1)
    def _():
        row = lax.broadcasted_iota(I32, (IDX_ROWS, LANES), 0)
        col = lax.broadcasted_iota(I32, (IDX_ROWS, LANES), 1)
        new_ok = jnp.logical_and(col <= row, col < t_new)
        keys_ref[0, n_pages] = jnp.where(new_ok, scores(knew_ref[0]), INT_MIN)


def _idx_scores_call(page_table, q2, w2, knew_t, kidx_t, t_new):
    db, n_pages = page_table.shape
    n_tiles = n_pages + 1
    bspec = lambda shape: pl.BlockSpec((1,) + shape, lambda b, i, pt: (b,) + (0,) * len(shape))
    return pl.pallas_call(
        functools.partial(_idx_scores_kernel, n_pages=n_pages, t_new=t_new),
        grid_spec=pltpu.PrefetchScalarGridSpec(
            num_scalar_prefetch=1,
            grid=(db, n_pages // PAGES_PER_STEP),
            in_specs=[bspec((N_IDX_HEADS * IDX_ROWS, D_IDX)), bspec((N_IDX_HEADS * IDX_ROWS, LANES)),
                      bspec((D_IDX, PAGE_SIZE))] + _page_specs(D_IDX),
            out_specs=bspec((n_tiles, IDX_ROWS, LANES))),
        out_shape=jax.ShapeDtypeStruct((db, n_tiles, IDX_ROWS, LANES), I32),
        compiler_params=pltpu.CompilerParams(dimension_semantics=("arbitrary", "arbitrary")),
        name="dsa_sample_scores",
    )(page_table, q2, w2, knew_t, *([kidx_t] * PAGES_PER_STEP))


def _idx_select_kernel(keys_ref, tri_ref, bias_ref, *, n_pages, k_top, t_new):
    keys = keys_ref[...]
    n_seq, n_tiles = keys.shape[:2]

    def count_where(pred):
        part = jnp.sum(jnp.where(pred(keys), 1.0, 0.0), axis=1, keepdims=True)
        return jnp.sum(part, axis=-1, keepdims=True)

    t_row = lax.broadcasted_iota(I32, (n_seq, 1, IDX_ROWS, 1), 2)
    n_valid = (jnp.minimum(t_row, t_new - 1) + (n_pages * PAGE_SIZE + 1)).astype(F32)
    thr, _ = _select_threshold(lambda t: count_where(lambda kv: kv >= t), n_valid, float(k_top))
    need = float(k_top) - count_where(lambda kv: kv > thr)

    eq = jnp.where(keys == thr, 1.0, 0.0).astype(MXU_DTYPE)
    pref = _dot(eq.reshape(n_seq * n_tiles * IDX_ROWS, LANES), tri_ref[...]).reshape(keys.shape)
    running = jnp.zeros((n_seq, 1, IDX_ROWS, 1), F32)
    for t in range(n_tiles):
        kv = keys[:, t:t + 1]
        pt = pref[:, t:t + 1]
        take_eq = jnp.where(running + pt <= need, 0.0, NEG_BIAS)
        bias_ref[:, t:t + 1] = jnp.where(kv > thr, 0.0, jnp.where(kv == thr, take_eq, NEG_BIAS))
        running = running + pt[..., LANES - 1:LANES]


def _idx_select_call(keys, n_pages, k_top, t_new):
    db, n_tiles = keys.shape[:2]
    seqs = min(SELECT_SEQS, db)
    tri = jnp.asarray((np.arange(LANES)[:, None] <= np.arange(LANES)[None, :]).astype(np.float32), MXU_DTYPE)
    spec = pl.BlockSpec((seqs, n_tiles, IDX_ROWS, LANES), lambda b: (b, 0, 0, 0))
    return pl.pallas_call(
        functools.partial(_idx_select_kernel, n_pages=n_pages, k_top=k_top, t_new=t_new),
        grid=(db // seqs,),
        in_specs=[spec, pl.BlockSpec((LANES, LANES), lambda b: (0, 0))],
        out_specs=spec,
        out_shape=jax.ShapeDtypeStruct(keys.shape, F32),
        compiler_params=pltpu.CompilerParams(
            dimension_semantics=("arbitrary",), vmem_limit_bytes=VMEM_LIMIT),
        name="dsa_sample_select",
    )(keys, tri)


def _paged_attn_kernel(pt_ref, q_ref, bias_ref, bias_new_ref, knew_ref, vnew_ref, *rest, t_new):
    kpages = rest[:PAGES_PER_STEP]
    vpages = rest[PAGES_PER_STEP:2 * PAGES_PER_STEP]
    out_ref, m_sc, l_sc, acc_sc = rest[2 * PAGES_PER_STEP:]
    i = pl.program_id(1)

    @pl.when(i == 0)
    def _():
        m_sc[...] = jnp.full(m_sc.shape, NEG_BIAS, F32)
        l_sc[...] = jnp.zeros(l_sc.shape, F32)
        acc_sc[...] = jnp.zeros(acc_sc.shape, F32)

    q = q_ref[0]

    def expand(b8):
        return jnp.concatenate(
            [jnp.broadcast_to(b8[t:t + 1, :], (N_HEADS_A, LANES)) for t in range(t_new)], axis=0)

    def step(k_tiles, v_tiles, biases):
        s = jnp.concatenate([_dot(q, kt) * HEAD_DIM_A ** -0.5 + expand(bb)
                             for kt, bb in zip(k_tiles, biases)], axis=1)
        m = m_sc[...]
        m_new = jnp.maximum(m, jnp.max(s, axis=-1, keepdims=True))
        alpha = jnp.exp(m - m_new)
        p = jnp.exp(s - m_new)
        l_sc[...] = alpha * l_sc[...] + jnp.sum(p, axis=-1, keepdims=True)
        pv = _dot_nt(p[:, :LANES].astype(MXU_DTYPE), v_tiles[0])
        for n, vt in enumerate(v_tiles[1:], 1):
            pv = pv + _dot_nt(p[:, n * LANES:(n + 1) * LANES].astype(MXU_DTYPE), vt)
        acc_sc[...] = alpha * acc_sc[...] + pv
        m_sc[...] = m_new

    step([kp[...].astype(MXU_DTYPE) for kp in kpages], [vp[...].astype(MXU_DTYPE) for vp in vpages],
         [bias_ref[0, gg] for gg in range(PAGES_PER_STEP)])

    @pl.when(i == pl.num_programs(1) - 1)
    def _():
        step([knew_ref[0]], [vnew_ref[0]], [bias_new_ref[0, 0]])
        out_ref[0] = acc_sc[...] / l_sc[...]


def _paged_attn_call(page_table, qp, bias, knew_t, vnew_t, k_t, v_t, t_new):
    db, n_pages = page_table.shape
    rows = t_new * N_HEADS_A
    bspec = lambda shape: pl.BlockSpec((1,) + shape, lambda b, i, pt: (b,) + (0,) * len(shape))
    return pl.pallas_call(
        functools.partial(_paged_attn_kernel, t_new=t_new),
        grid_spec=pltpu.PrefetchScalarGridSpec(
            num_scalar_prefetch=1,
            grid=(db, n_pages // PAGES_PER_STEP),
            in_specs=[bspec((rows, KV_W)),
                      pl.BlockSpec((1, PAGES_PER_STEP, IDX_ROWS, LANES), lambda b, i, pt: (b, i, 0, 0)),
                      pl.BlockSpec((1, 1, IDX_ROWS, LANES), lambda b, i, pt: (b, n_pages, 0, 0)),
                      bspec((KV_W, PAGE_SIZE)), bspec((KV_W, PAGE_SIZE))]
                     + _page_specs(KV_W) + _page_specs(KV_W),
            out_specs=bspec((rows, KV_W)),
            scratch_shapes=[pltpu.VMEM((rows, 1), F32), pltpu.VMEM((rows, 1), F32),
                            pltpu.VMEM((rows, KV_W), F32)]),
        out_shape=jax.ShapeDtypeStruct((db, rows, KV_W), F32),
        compiler_params=pltpu.CompilerParams(dimension_semantics=("arbitrary", "arbitrary")),
        name="dsa_sample_attend",
    )(page_table, qp, bias, bias, knew_t, vnew_t, *([k_t] * PAGES_PER_STEP), *([v_t] * PAGES_PER_STEP))


def _layout_w_in(w_in):
    cuts = np.cumsum(PROJ_SIZES)[:-1].tolist()
    wq, wk, wv, wiq, wik, wiw, wrq, wrk, wrv, wrg = jnp.split(w_in, cuts, axis=1)
    ikw = jnp.pad(jnp.concatenate([wik, wiw], axis=1), ((0, 0), (0, LANES - D_IDX - N_IDX_HEADS)))
    w = jnp.concatenate([wq, wk, wv, wiq, ikw, wrq, wrk, wrv, wrg], axis=1)
    assert w.shape[1] == W_PROJ
    return w.astype(MXU_DTYPE)


def _rope_tables(pos):
    posf = pos.astype(F32)[:, None]

    def tables(rot_dim, theta, period, width):
        half = rot_dim // 2
        inv = theta ** (-jnp.arange(half, dtype=F32) / half)
        ang = posf * inv[None, :]
        cos, sin = jnp.cos(ang), jnp.sin(ang)
        n = pos.shape[0]
        rest = period - rot_dim
        c = jnp.concatenate([cos, cos, jnp.ones((n, rest), F32)], axis=1)
        su = jnp.concatenate([-sin, jnp.zeros((n, half + rest), F32)], axis=1)
        sd = jnp.concatenate([jnp.zeros((n, half), F32), sin, jnp.zeros((n, rest), F32)], axis=1)
        return [jnp.tile(t, (1, width // period)) for t in (c, su, sd)]

    a = tables(ROT_DIM_A, ROPE_THETA, HEAD_DIM_A, LANES)
    r = tables(DK_R, RET_THETA, DK_R, LANES)
    return jnp.concatenate(a + r, axis=1)


def kernel(x_prompt, x_sample, cache_k, cache_v, cache_kidx, state_ret, page_table, w_in, w_out, w_up, w_down,
           g_pre_mix, g_post_mix, g_pre_mlp, g_post_mlp, kidx_ln_g, kidx_ln_b, ret_gn_g, ret_gn_b):
    b, s, d = x_prompt.shape
    db, t_new, _ = x_sample.shape
    n_pages = page_table.shape[1]
    past = n_pages * PAGE_SIZE
    layer = 0
    assert w_in.shape[0] == 1 and t_new <= IDX_ROWS and n_pages % PAGES_PER_STEP == 0
    assert s % (2 * KEY_TILE) == 0 and (db * SAMPLE_ROWS) % KEY_TILE == 0

    w_proj = _layout_w_in(w_in[layer])
    wo, wu, wd = (w_[layer].astype(MXU_DTYPE) for w_ in (w_out, w_up, w_down))
    row = lambda v: v[layer][None, :].astype(F32)
    g_pre, g1, g2, g3 = row(g_pre_mix), row(g_post_mix), row(g_pre_mlp), row(g_post_mlp)
    lng = jnp.pad(row(kidx_ln_g), ((0, 0), (0, LANES - D_IDX)))
    lnb = jnp.pad(row(kidx_ln_b), ((0, 0), (0, LANES - D_IDX)))
    gng, gnb = row(ret_gn_g), row(ret_gn_b)

    tm = 2 * KEY_TILE
    pp = _proj_call(x_prompt, _rope_tables(jnp.arange(s)), g_pre, w_proj, lng, lnb, tm)
    a_out = _dsa_call(pp["qa"], pp["iq"], pp["iwT"], pp["ikd"], pp["kb"], pp["vTb"])
    zero_state = jnp.zeros((b, N_HEADS_R, DK_R, DV_R), F32)
    r_out, ret_prompt = _ret_call(pp["rq"], pp["rk"], pp["rv"], pp["rg"], zero_state, gng, gnb,
                                  RET_CHUNK, float(RET_CHUNK))
    y_prompt = _mlp_call(x_prompt.reshape(b * s, d), a_out.reshape(b * s, 512), r_out.reshape(b * s, 512),
                         wo, wu, wd, g1, g2, g3, tm).reshape(b, s, d)
    kv_out = lambda t: t.reshape(b, N_KV_A, HEAD_DIM_A, s).transpose(0, 3, 1, 2)[None]
    k_prompt, v_prompt = kv_out(pp["kT"]), kv_out(pp["vT"])
    kidx_prompt = pp["ikT"].transpose(0, 2, 1)[None]

    sr = SAMPLE_ROWS
    n_s = db * sr
    xs = jnp.pad(x_sample, ((0, 0), (0, sr - t_new), (0, 0))).reshape(1, n_s, d)
    pos_s = jnp.tile(past + jnp.arange(sr), db)
    tm_s = min(2 * KEY_TILE, n_s)
    ps = _proj_call(xs, _rope_tables(pos_s), g_pre, w_proj, lng, lnb, tm_s)
    per_seq = lambda t: t.reshape(db, sr, t.shape[-1])
    per_seq_t = lambda t: t.reshape(t.shape[1], db, sr).transpose(1, 0, 2)
    k_s, v_s, ik_s, iw_s = (per_seq_t(ps[n_]) for n_ in ("kT", "vT", "ikT", "iwT"))

    k_top = min(TOPK_MAX, (past + t_new) // 4)
    iq_s = per_seq(ps["iq"])[:, :IDX_ROWS].reshape(db, IDX_ROWS, N_IDX_HEADS, D_IDX)
    q2 = iq_s.transpose(0, 2, 1, 3).reshape(db, N_IDX_HEADS * IDX_ROWS, D_IDX)
    w2 = jnp.broadcast_to(iw_s[:, :, :IDX_ROWS].reshape(db, N_IDX_HEADS * IDX_ROWS, 1),
                          (db, N_IDX_HEADS * IDX_ROWS, LANES))
    pad_slots = lambda t: jnp.pad(t, ((0, 0), (0, 0), (0, PAGE_SIZE - sr))).astype(MXU_DTYPE)
    kidx_t = cache_kidx[layer].transpose(0, 2, 1)
    k_t = cache_k[layer].transpose(0, 2, 3, 1).reshape(-1, KV_W, PAGE_SIZE)
    v_t = cache_v[layer].transpose(0, 2, 3, 1).reshape(-1, KV_W, PAGE_SIZE)
    keys = _idx_scores_call(page_table, q2, w2, pad_slots(ik_s), kidx_t, t_new)
    bias = _idx_select_call(keys, n_pages, k_top, t_new)

    qa_s = per_seq(ps["qa"])[:, :t_new].reshape(db, t_new, N_HEADS_A, HEAD_DIM_A)
    grp = (jnp.arange(N_HEADS_A) // (N_HEADS_A // N_KV_A))[None, None, :, None, None]
    own = grp == jnp.arange(N_KV_A)[None, None, None, :, None]
    qp = jnp.where(own, qa_s[:, :, :, None, :], jnp.zeros((), MXU_DTYPE)).reshape(db, t_new * N_HEADS_A, KV_W)
    o_s = _paged_attn_call(page_table, qp, bias, pad_slots(k_s), pad_slots(v_s), k_t, v_t, t_new)
    o_s = o_s.reshape(db, t_new, N_HEADS_A, N_KV_A, HEAD_DIM_A)
    a_s = jnp.where(own, o_s, 0.0).sum(axis=3)
    a_s = jnp.pad(a_s.reshape(db, t_new, 512), ((0, 0), (0, sr - t_new), (0, 0))).astype(MXU_DTYPE)

    r_s, ret_sample = _ret_call(per_seq(ps["rq"]), per_seq(ps["rk"]), per_seq(ps["rv"]), per_seq(ps["rg"]),
                                state_ret[layer].astype(F32), gng, gnb, sr, float(t_new))
    y_s = _mlp_call(xs.reshape(n_s, d), a_s.reshape(n_s, 512), r_s.reshape(n_s, 512),
                    wo, wu, wd, g1, g2, g3, tm_s).reshape(db, sr, d)[:, :t_new]

    kv_s_out = lambda t: t[:, :, :t_new].reshape(db, N_KV_A, HEAD_DIM_A, t_new).transpose(0, 3, 1, 2)[None]
    return (y_prompt, y_s, k_prompt, v_prompt, kidx_prompt, ret_prompt[None],
            kv_s_out(k_s), kv_s_out(v_s), ik_s[:, :, :t_new].transpose(0, 2, 1)[None], ret_sample[None])
```

```python
import functools

import numpy as np
import jax
import jax.numpy as jnp
from jax import lax
from jax.experimental import pallas as pl
from jax.experimental.pallas import tpu as pltpu

F32 = jnp.float32
I32 = jnp.int32
MXU_DTYPE = jnp.bfloat16

LANES = 128
SUBLANES = 8
N_HEADS_A = 8
N_KV_A = 2
HEAD_DIM_A = 64
ROT_DIM_A = 16
ROPE_THETA = 500000.0
N_IDX_HEADS = 8
D_IDX = 64
TOPK_MAX = 256
N_HEADS_R = 4
DV_R = 128
DK_R = 64
RET_THETA = 10000.0
RET_CHUNK = 128
PAGE_SIZE = 128
EPS = 1e-6
PROJ_SIZES = (N_HEADS_A * HEAD_DIM_A, N_KV_A * HEAD_DIM_A, N_KV_A * HEAD_DIM_A,
              N_IDX_HEADS * D_IDX, D_IDX, N_IDX_HEADS,
              N_HEADS_R * DK_R, N_HEADS_R * DK_R, N_HEADS_R * DV_R, N_HEADS_R * DV_R)
KV_W = N_KV_A * HEAD_DIM_A

INT_MIN = -(2 ** 31)
NEG_BIAS = -1e30
SAMPLE_ROWS = 16
KEY_TILE = 256
RET_SEQS = 4
VMEM_LIMIT = 56 * 1024 * 1024

_SEG = {}
_off = 0
for _name, _w in (("q", 512), ("k", 128), ("v", 128), ("iq", 512),
                  ("ikw", 128), ("rq", 256), ("rk", 256), ("rv", 512), ("rg", 512)):
    _SEG[_name] = (_off, _off + _w)
    _off += _w
W_PROJ = _off


def _dot(a, b):
    return jnp.dot(a, b, preferred_element_type=F32)


def _dot_nt(a, b):
    return lax.dot_general(a, b, (((1,), (1,)), ((), ())), preferred_element_type=F32)


def _dot_tn(a, b):
    return lax.dot_general(a, b, (((0,), (0,)), ((), ())), preferred_element_type=F32)


def _tile_lanes(t, width):
    reps = width // t.shape[1]
    return t if reps == 1 else jnp.concatenate([t] * reps, axis=1)


def _tree_sum(parts):
    while len(parts) > 1:
        parts = [a + b for a, b in zip(parts[::2], parts[1::2])]
    return parts[0]


MASKED_SCORE = jnp.nan
LOWEST_KEY = -0x7F7FFFFF
MIN_NORMAL_BITS = 0x00800000


def _threshold_value(t):
    pos = jnp.where(jnp.logical_and(t > 0, t < MIN_NORMAL_BITS), MIN_NORMAL_BITS, t)
    return jnp.where(t >= 0, pltpu.bitcast(pos, F32), -pltpu.bitcast(jnp.abs(t), F32))


def _floor_to_bf16(x):
    r = x.astype(jnp.bfloat16)
    rf = r.astype(F32)
    down = (rf * jnp.where(rf > 0.0, 1.0 - 2.0 ** -8, 1.0 + 2.0 ** -8 + 2.0 ** -16)).astype(jnp.bfloat16)
    return jnp.where(rf > x, down, r)


def _coarse_threshold(t):
    t16 = lax.shift_right_arithmetic(t, 16)
    pos = jnp.where(jnp.logical_and(t16 > 0, t16 < 0x80), 0x80, t16)
    mag = pltpu.bitcast(lax.shift_left(jnp.where(t16 >= 0, pos, -t16), 16), F32)
    return jnp.where(t16 >= 0, mag, -mag).astype(jnp.bfloat16)


def _select_threshold(count_ge, n_valid, k_top, count_ge_coarse=None):
    def refine(count_fn, state, n_bits):
        def step(_, c):
            bit, u, cnt = c
            cand = u | lax.shift_left(jnp.int32(1), bit)
            c_new = count_fn(cand ^ INT_MIN)
            ok = c_new >= k_top
            return bit - 1, jnp.where(ok, cand, u), jnp.where(ok, c_new, cnt)

        return lax.fori_loop(0, n_bits, step, state)

    state = (jnp.int32(31), jnp.zeros(n_valid.shape, I32), n_valid)
    if count_ge_coarse is not None:
        state = refine(count_ge_coarse, state, 16)
    _, u, cnt = refine(count_ge, state, 32 if count_ge_coarse is None else 16)
    return _threshold_value(jnp.maximum(u ^ INT_MIN, LOWEST_KEY)), cnt


def _rope(z, cos_t, sin_up_t, sin_dn_t, half):
    w = z.shape[1]
    up = pltpu.roll(z, w - half, 1)
    dn = pltpu.roll(z, half, 1)
    return z * _tile_lanes(cos_t, w) + up * _tile_lanes(sin_up_t, w) + dn * _tile_lanes(sin_dn_t, w)


def _proj_kernel(x_ref, tab_ref, g_ref, w_ref, lng_ref, lnb_ref,
                 qa_o, kb_o, iq_o, ikd_o, rq_o, rk_o, rv_o, rg_o, kt_o, vt_o, vtb_o, ikt_o, iwt_o):
    half_a = ROT_DIM_A // 2
    half_r = DK_R // 2
    for r in range(x_ref.shape[1] // KEY_TILE):
        rows = slice(r * KEY_TILE, (r + 1) * KEY_TILE)
        x = x_ref[0, rows, :]
        h = (x * lax.rsqrt(jnp.mean(x * x, axis=-1, keepdims=True) + EPS) * g_ref[...]).astype(MXU_DTYPE)

        def seg(name, h=h):
            lo, hi = _SEG[name]
            return _dot(h, w_ref[:, lo:hi])

        tab = tab_ref[rows, :]
        c_a, su_a, sd_a, c_r, su_r, sd_r = [tab[:, i * LANES:(i + 1) * LANES] for i in range(6)]

        qa_o[0, rows, :] = _rope(seg("q"), c_a, su_a, sd_a, half_a).astype(qa_o.dtype)
        iq_o[0, rows, :] = _rope(seg("iq"), c_a, su_a, sd_a, half_a).astype(iq_o.dtype)
        rq_o[0, rows, :] = _rope(seg("rq"), c_r, su_r, sd_r, half_r).astype(rq_o.dtype)
        rk_o[0, rows, :] = _rope(seg("rk") * DK_R ** -0.5, c_r, su_r, sd_r, half_r).astype(rk_o.dtype)
        rv_o[0, rows, :] = seg("rv").astype(rv_o.dtype)
        rg_o[0, rows, :] = seg("rg")

        k = _rope(seg("k"), c_a, su_a, sd_a, half_a)
        kb_o[0, rows, :] = k.astype(kb_o.dtype)
        kt_o[0, :, rows] = k.T
        vt = seg("v").T
        vt_o[0, :, rows] = vt
        vtb_o[0, r] = vt.astype(vtb_o.dtype)

        zi = seg("ikw")
        lane = lax.broadcasted_iota(I32, zi.shape, 1)
        is_ik = lane < D_IDX
        mu = jnp.sum(jnp.where(is_ik, zi, 0.0), axis=-1, keepdims=True) * (1.0 / D_IDX)
        d = jnp.where(is_ik, zi - mu, 0.0)
        var = jnp.sum(d * d, axis=-1, keepdims=True) * (1.0 / D_IDX)
        y = d * lax.rsqrt(var + EPS) * lng_ref[...] + lnb_ref[...]
        ikr = jnp.where(is_ik, _rope(y, c_a, su_a, sd_a, half_a), 0.0)
        ikd_o[0, rows, :] = (ikr + pltpu.roll(ikr, D_IDX, 1)).astype(ikd_o.dtype)
        both_t = jnp.where(is_ik, ikr, zi * (N_IDX_HEADS * D_IDX) ** -0.5).T
        ikt_o[0, :, rows] = both_t[:D_IDX]
        iwt_o[0, :, rows] = both_t[D_IDX:D_IDX + N_IDX_HEADS]


def _proj_call(x3, tab, g, w, lng, lnb, tm):
    bx, n, d = x3.shape
    grid = (n // tm, bx)
    tspec = lambda w_: pl.BlockSpec((1, tm, w_), lambda i, b: (b, i, 0))
    ttspec = lambda r_: pl.BlockSpec((1, r_, tm), lambda i, b: (b, 0, i))
    const = lambda shape: pl.BlockSpec(shape, lambda i, b: (0,) * len(shape))
    nat = (("qa", 512, MXU_DTYPE), ("kb", KV_W, MXU_DTYPE), ("iq", 512, MXU_DTYPE), ("ikd", 128, MXU_DTYPE),
           ("rq", 256, MXU_DTYPE), ("rk", 256, MXU_DTYPE), ("rv", 512, MXU_DTYPE), ("rg", 512, F32))
    tr = (("kT", KV_W, F32), ("vT", KV_W, F32))
    tr2 = (("ikT", D_IDX, F32), ("iwT", N_IDX_HEADS, F32))
    out_specs = ([tspec(w_) for _, w_, _ in nat] + [ttspec(r_) for _, r_, _ in tr]
                 + [pl.BlockSpec((1, tm // KEY_TILE, KV_W, KEY_TILE), lambda i, b: (b, i, 0, 0))]
                 + [ttspec(r_) for _, r_, _ in tr2])
    out_shape = ([jax.ShapeDtypeStruct((bx, n, w_), dt) for _, w_, dt in nat]
                 + [jax.ShapeDtypeStruct((bx, r_, n), dt) for _, r_, dt in tr]
                 + [jax.ShapeDtypeStruct((bx, n // KEY_TILE, KV_W, KEY_TILE), MXU_DTYPE)]
                 + [jax.ShapeDtypeStruct((bx, r_, n), dt) for _, r_, dt in tr2])
    names = [o[0] for o in nat] + [o[0] for o in tr] + ["vTb"] + [o[0] for o in tr2]
    res = pl.pallas_call(
        _proj_kernel,
        grid=grid,
        in_specs=[tspec(d),
                  pl.BlockSpec((tm, 6 * LANES), lambda i, b: (i, 0)),
                  const((1, d)), const((d, W_PROJ)), const((1, LANES)), const((1, LANES))],
        out_specs=tuple(out_specs),
        out_shape=tuple(out_shape),
        compiler_params=pltpu.CompilerParams(
            dimension_semantics=("arbitrary", "arbitrary"), vmem_limit_bytes=VMEM_LIMIT),
        name="proj",
    )(x3, tab, g, w, lng, lnb)
    return dict(zip(names, res))


def _dsa_kernel(qa_ref, iq_ref, iwt_ref, iqn_ref, iwtn_ref, ikd_ref, kb_ref, vtb_ref, tri_ref, out_ref,
                keys_sc, top_sc, bias_sc, iqp_sc, qap_sc, acc_sc, *, k_top):
    tq = tk = KEY_TILE
    j = pl.program_id(1)
    q0 = j * tq
    nkt = j + 1
    heads_per_kv = N_HEADS_A // N_KV_A

    lane = lax.broadcasted_iota(I32, (tq, LANES), 1)
    lo_half = lane < HEAD_DIM_A
    hi_half = jnp.logical_not(lo_half)

    def load_indexer_queries(src_ref):
        for h in range(N_IDX_HEADS):
            sl = slice((h // 2) * LANES, (h // 2 + 1) * LANES)
            iqp_sc[h] = jnp.where(lo_half if h % 2 == 0 else hi_half, src_ref[0, :, sl], jnp.zeros((), MXU_DTYPE))

    for h in range(N_HEADS_A):
        sl = slice((h // 2) * LANES, (h // 2 + 1) * LANES)
        g = h // heads_per_kv
        q = qa_ref[0, :, sl].astype(F32) * HEAD_DIM_A ** -0.5
        if h % 2 != g:
            q = pltpu.roll(q, HEAD_DIM_A, 1)
        qap_sc[h] = jnp.where(lo_half if g == 0 else hi_half, q, 0.0).astype(MXU_DTYPE)

    kpos = lax.broadcasted_iota(I32, (tk, tq), 0)
    qcol = lax.broadcasted_iota(I32, (tk, tq), 1)

    def score_tile(kt, qstart, w_ref):
        k0 = pl.multiple_of(kt * tk, tk)
        kb = ikd_ref[0, pl.ds(k0, tk), :]
        acc = jnp.zeros((tk, tq), F32)
        for h in range(N_IDX_HEADS):
            acc = acc + jnp.maximum(_dot_nt(kb, iqp_sc[h]), 0.0) * w_ref[0, h:h + 1, :]
        causal = kpos + k0 <= qcol + qstart
        keys_sc[kt] = jnp.where(causal, acc, MASKED_SCORE)
        top_sc[kt] = jnp.where(causal, _floor_to_bf16(acc), jnp.nan)

    @pl.when(j == 0)
    def _():
        load_indexer_queries(iq_ref)
        score_tile(0, 0, iwt_ref)

    def count_where(pred):
        def body(kt, part):
            hit = jnp.where(pred(keys_sc[kt]), 1.0, 0.0)
            return part + _tree_sum([hit[r * SUBLANES:(r + 1) * SUBLANES] for r in range(tk // SUBLANES)])
        part = lax.fori_loop(0, nkt, body, jnp.zeros((SUBLANES, tq), F32))
        return jnp.sum(part, axis=0, keepdims=True)

    def count_coarse(t):
        tb = _coarse_threshold(t)
        rows = 2 * SUBLANES

        def body(kt, part):
            hit = jnp.where(top_sc[kt] >= tb, jnp.ones((), top_sc.dtype), jnp.zeros((), top_sc.dtype))
            return part + _tree_sum([hit[r * rows:(r + 1) * rows] for r in range(tk // rows)]).astype(F32)
        part = lax.fori_loop(0, nkt, body, jnp.zeros((rows, tq), F32))
        return jnp.sum(part, axis=0, keepdims=True)

    n_valid = (lax.broadcasted_iota(I32, (1, tq), 1) + (q0 + 1)).astype(F32)
    def count_ge(t):
        tv = _threshold_value(t)
        return count_where(lambda kv: kv >= tv)

    thr, n_ge = _select_threshold(count_ge, n_valid, float(k_top), count_ge_coarse=count_coarse)
    has_ties = jnp.max(n_ge) > float(k_top)

    @pl.when(jnp.logical_not(has_ties))
    def _():
        def bias_body(kt, carry):
            bias_sc[kt] = jnp.where(keys_sc[kt] >= thr, 0.0, NEG_BIAS)
            return carry
        lax.fori_loop(0, nkt, bias_body, 0)

    @pl.when(has_ties)
    def _():
        need = float(k_top) - count_where(lambda kv: kv > thr)

        def bias_body(kt, running):
            kv = keys_sc[kt]
            eq = kv == thr
            pref = _dot(tri_ref[...], jnp.where(eq, 1.0, 0.0).astype(MXU_DTYPE))
            take_eq = jnp.where(running + pref <= need, 0.0, NEG_BIAS)
            bias_sc[kt] = jnp.where(kv > thr, 0.0, jnp.where(eq, take_eq, NEG_BIAS))
            return running + pref[tk - 1:tk, :]

        lax.fori_loop(0, nkt, bias_body, jnp.zeros((1, tq), F32))

    wide = heads_per_kv * tq
    acc_sc[...] = jnp.zeros(acc_sc.shape, F32)

    @pl.when(nkt % 2 == 1)
    def _():
        bias_sc[nkt] = jnp.full((tk, tq), NEG_BIAS, F32)

    load_indexer_queries(iqn_ref)
    q0_next = q0 + tq

    def att_body(kp, carry):
        k0 = pl.multiple_of(kp * (2 * tk), 2 * tk)
        bias = _tile_lanes(bias_sc[pl.ds(2 * kp, 2)].reshape(2 * tk, tq), wide)
        kb = kb_ref[0, pl.ds(k0, 2 * tk), :]
        out = []
        for g, (m, l) in enumerate(carry):
            qg = qap_sc[g * heads_per_kv:(g + 1) * heads_per_kv].reshape(wide, LANES)
            gsl = slice(g * HEAD_DIM_A, (g + 1) * HEAD_DIM_A)
            vb = jnp.concatenate([vtb_ref[0, 2 * kp, gsl, :], vtb_ref[0, 2 * kp + 1, gsl, :]], axis=1)
            s = _dot_nt(kb, qg) + bias
            m_new = jnp.maximum(m, jnp.max(s, axis=0, keepdims=True))
            alpha = jnp.exp(m - m_new)
            p = jnp.exp(s - m_new)
            acc_sc[g] = alpha * acc_sc[g] + _dot(vb, p.astype(MXU_DTYPE))
            psum = jnp.sum(p.reshape(2 * tk // SUBLANES, SUBLANES, wide), axis=0)
            out.append((m_new, alpha * l + psum))
            score_tile(2 * kp + g, q0_next, iwtn_ref)
        return tuple(out)

    init = tuple((jnp.full((1, wide), NEG_BIAS, F32), jnp.zeros((SUBLANES, wide), F32)) for _ in range(N_KV_A))
    fin = lax.fori_loop(0, (nkt + 1) // 2, att_body, init)
    for g, (_, l) in enumerate(fin):
        acc_sc[g] = acc_sc[g] / jnp.sum(l, axis=0, keepdims=True)

    @pl.when(jnp.logical_and(nkt % 2 == 0, j + 1 < pl.num_programs(1)))
    def _():
        score_tile(nkt, q0_next, iwtn_ref)

    for hp in range(N_HEADS_A // 2):
        g, n = divmod(2 * hp, heads_per_kv)
        pair = jnp.concatenate([acc_sc[g, :, n * tq:(n + 1) * tq],
                                acc_sc[g, :, (n + 1) * tq:(n + 2) * tq]], axis=0)
        out_ref[0, :, hp * LANES:(hp + 1) * LANES] = pair.T.astype(out_ref.dtype)


def _dsa_call(qa, iq, iwt, ikd, kb, vtb):
    b, s, _ = qa.shape
    tq = KEY_TILE
    nq = s // tq
    k_top = min(TOPK_MAX, s // 4)
    tri = jnp.asarray((np.arange(tq)[:, None] >= np.arange(tq)[None, :]).astype(np.float32), MXU_DTYPE)
    qspec = lambda w_: pl.BlockSpec((1, tq, w_), lambda bb, jj: (bb, jj, 0))
    sspec = lambda w_: pl.BlockSpec((1, s, w_), lambda bb, jj: (bb, 0, 0))
    return pl.pallas_call(
        functools.partial(_dsa_kernel, k_top=k_top),
        grid=(b, nq),
        in_specs=[qspec(512), qspec(512),
                  pl.BlockSpec((1, N_IDX_HEADS, tq), lambda bb, jj: (bb, 0, jj)),
                  pl.BlockSpec((1, tq, 512), lambda bb, jj: (bb, jnp.minimum(jj + 1, nq - 1), 0)),
                  pl.BlockSpec((1, N_IDX_HEADS, tq), lambda bb, jj: (bb, 0, jnp.minimum(jj + 1, nq - 1))),
                  sspec(128), sspec(KV_W),
                  pl.BlockSpec((1, nq, KV_W, tq), lambda bb, jj: (bb, 0, 0, 0)),
                  pl.BlockSpec((tq, tq), lambda bb, jj: (0, 0))],
        out_specs=qspec(512),
        out_shape=jax.ShapeDtypeStruct((b, s, 512), MXU_DTYPE),
        scratch_shapes=[pltpu.VMEM((nq, tq, tq), F32), pltpu.VMEM((nq, tq, tq), jnp.bfloat16),
                        pltpu.VMEM((nq, tq, tq), F32),
                        pltpu.VMEM((N_IDX_HEADS, tq, LANES), MXU_DTYPE),
                        pltpu.VMEM((N_HEADS_A, tq, LANES), MXU_DTYPE),
                        pltpu.VMEM((N_KV_A, HEAD_DIM_A, (N_HEADS_A // N_KV_A) * tq), F32)],
        compiler_params=pltpu.CompilerParams(
            dimension_semantics=("arbitrary", "arbitrary"), vmem_limit_bytes=VMEM_LIMIT),
        name="dsa_prompt",
    )(qa, iq, iwt, iq, iwt, ikd, kb, vtb, tri)


def _ret_kernel(rq_ref, rk_ref, rv_ref, rg_ref, st_ref, dmask_ref, cross_ref, kdec_ref, gdec_ref,
                gng_ref, gnb_ref, out_ref, st_out_ref, st_sc):
    c = pl.program_id(1)
    n_seq = rq_ref.shape[0]

    @pl.when(c == 0)
    def _():
        st_sc[...] = jnp.zeros(st_sc.shape, F32)
        for i in range(n_seq):
            for h in range(N_HEADS_R):
                off = (h % 2) * DK_R
                st_sc[i, h, off:off + DK_R, :] = st_ref[i, h]

    lane = lax.broadcasted_iota(I32, (rq_ref.shape[1], LANES), 1)
    for i in range(n_seq):
        for h in range(N_HEADS_R):
            sl = slice(h * LANES, (h + 1) * LANES)
            psl = slice((h // 2) * LANES, (h // 2 + 1) * LANES)
            own = (lane < DK_R) if h % 2 == 0 else (lane >= DK_R)
            q = jnp.where(own, rq_ref[i, :, psl], jnp.zeros((), MXU_DTYPE))
            k = jnp.where(own, rk_ref[i, :, psl], jnp.zeros((), MXU_DTYPE))
            v = rv_ref[i, :, sl]
            state = st_sc[i, h]
            inner = _dot_nt(q, k) * dmask_ref[h]
            o = _dot(inner.astype(MXU_DTYPE), v) + _dot(q, state.astype(MXU_DTYPE)) * cross_ref[:, sl]
            kd = (k.astype(F32) * kdec_ref[:, psl]).astype(MXU_DTYPE)
            st_sc[i, h] = state * gdec_ref[h] + _dot_tn(kd, v)

            mu = jnp.mean(o, axis=-1, keepdims=True)
            d = o - mu
            var = jnp.mean(d * d, axis=-1, keepdims=True)
            y = d * lax.rsqrt(var + EPS) * gng_ref[:, sl] + gnb_ref[:, sl]
            gate = rg_ref[i, :, sl]
            out_ref[i, :, sl] = (gate * (1.0 / (1.0 + jnp.exp(-gate))) * y).astype(out_ref.dtype)

    @pl.when(c == pl.num_programs(1) - 1)
    def _():
        for i in range(n_seq):
            for h in range(N_HEADS_R):
                off = (h % 2) * DK_R
                st_out_ref[i, h] = st_sc[i, h, off:off + DK_R, :]


def _ret_tables(cr, c_eff):
    log_g = jnp.log(1.0 - 2.0 ** (-5.0 - jnp.arange(N_HEADS_R, dtype=F32)))
    i = jnp.arange(cr, dtype=F32)
    diff = i[:, None] - i[None, :]
    dmask = jnp.where(diff >= 0, jnp.exp(jnp.maximum(diff, 0.0)[None] * log_g[:, None, None]), 0.0)
    cross = jnp.exp((i + 1.0)[:, None] * log_g[None, :])
    kdec = jnp.where((i < c_eff)[:, None], jnp.exp((c_eff - 1.0 - i)[:, None] * log_g[None, :]), 0.0)
    gdec = jnp.exp(c_eff * log_g)
    return (dmask, jnp.repeat(cross, DV_R, axis=1), jnp.repeat(kdec, DK_R, axis=1),
            jnp.broadcast_to(gdec[:, None, None], (N_HEADS_R, 1, LANES)))


def _ret_call(rq, rk, rv, rg, state, gng, gnb, cr, c_eff):
    b, s, _ = rq.shape
    nc = s // cr
    dmask, cross, kdec, gdec = _ret_tables(cr, c_eff)
    kw = N_HEADS_R * DK_R
    ns = max(n for n in (RET_SEQS, 2, 1) if b % n == 0)
    tspec = pl.BlockSpec((ns, cr, 512), lambda bb, cc: (bb, cc, 0))
    kspec = pl.BlockSpec((ns, cr, kw), lambda bb, cc: (bb, cc, 0))
    sspec = pl.BlockSpec((ns, N_HEADS_R, DK_R, DV_R), lambda bb, cc: (bb, 0, 0, 0))
    const = lambda shape: pl.BlockSpec(shape, lambda bb, cc: (0,) * len(shape))
    return pl.pallas_call(
        _ret_kernel,
        grid=(b // ns, nc),
        in_specs=[kspec, kspec, tspec, tspec, sspec,
                  const((N_HEADS_R, cr, cr)), const((cr, 512)), const((cr, kw)),
                  const((N_HEADS_R, 1, LANES)), const((1, 512)), const((1, 512))],
        out_specs=(tspec, sspec),
        out_shape=(jax.ShapeDtypeStruct((b, s, 512), MXU_DTYPE),
                   jax.ShapeDtypeStruct((b, N_HEADS_R, DK_R, DV_R), F32)),
        scratch_shapes=[pltpu.VMEM((ns, N_HEADS_R, LANES, DV_R), F32)],
        compiler_params=pltpu.CompilerParams(dimension_semantics=("arbitrary", "arbitrary")),
        name="retention",
    )(rq, rk, rv, rg, state, dmask, cross, kdec, gdec, gng, gnb)


def _mlp_kernel(x_ref, a_ref, r_ref, wo_ref, wu_ref, wd_ref, g1_ref, g2_ref, g3_ref, out_ref, *, ff_chunk):
    def rms(t, g_ref):
        return t * lax.rsqrt(jnp.mean(t * t, axis=-1, keepdims=True) + EPS) * g_ref[...]

    half = a_ref.shape[1]
    mix = _dot(a_ref[...], wo_ref[:half, :]) + _dot(r_ref[...], wo_ref[half:, :])
    x1 = x_ref[...] + rms(mix, g1_ref)
    h2 = rms(x1, g2_ref).astype(MXU_DTYPE)
    acc = jnp.zeros(x1.shape, F32)
    for c in range(wu_ref.shape[1] // ff_chunk):
        sl = slice(c * ff_chunk, (c + 1) * ff_chunk)
        u = jnp.maximum(_dot(h2, wu_ref[:, sl]), 0.0)
        acc = acc + _dot((u * u).astype(MXU_DTYPE), wd_ref[sl, :])
    out_ref[...] = x1 + rms(acc, g3_ref)


def _mlp_call(x2, a, r, wo, wu, wd, g1, g2, g3, tm):
    n, d = x2.shape
    dff = wu.shape[1]
    tspec = lambda w_: pl.BlockSpec((tm, w_), lambda i: (i, 0))
    const = lambda shape: pl.BlockSpec(shape, lambda i: (0, 0))
    return pl.pallas_call(
        functools.partial(_mlp_kernel, ff_chunk=1024),
        grid=(n // tm,),
        in_specs=[tspec(d), tspec(512), tspec(512), const((d, d)), const((d, dff)), const((dff, d)),
                  const((1, d)), const((1, d)), const((1, d))],
        out_specs=tspec(d),
        out_shape=jax.ShapeDtypeStruct((n, d), F32),
        compiler_params=pltpu.CompilerParams(
            dimension_semantics=("arbitrary",), vmem_limit_bytes=VMEM_LIMIT),
        name="mix_mlp",
    )(x2, a, r, wo, wu, wd, g1, g2, g3)


IDX_ROWS = 8
SELECT_SEQS = 16


def _page_gather(pt_ref, cache_hbm, buf, sem, n_pages):
    b = pl.program_id(0)
    slot = lax.rem(b, 2)

    def copy(seq, page_no, sl):
        return pltpu.make_async_copy(cache_hbm.at[pt_ref[seq, page_no]], buf.at[sl, page_no], sem.at[sl])

    def start_all(seq, sl):
        for p in range(n_pages):
            copy(seq, p, sl).start()

    @pl.when(b == 0)
    def _():
        start_all(0, 0)

    @pl.when(b + 1 < pl.num_programs(0))
    def _():
        start_all(b + 1, 1 - slot)

    for p in range(n_pages):
        copy(b, p, slot).wait()
    return slot


def _idx_scores_kernel(pt_ref, q_ref, w_ref, knew_ref, kidx_hbm, keys_ref, buf, sem, *, n_pages, t_new):
    slot = _page_gather(pt_ref, kidx_hbm, buf, sem, n_pages)
    q = q_ref[0]
    w = w_ref[0]

    def scores(kpage_t):
        s = jnp.maximum(_dot(q, kpage_t), 0.0) * w
        acc = s[0:IDX_ROWS]
        for h in range(1, N_IDX_HEADS):
            acc = acc + s[h * IDX_ROWS:(h + 1) * IDX_ROWS]
        return acc

    for p in range(n_pages):
        keys_ref[0, p] = scores(buf[slot, p].astype(MXU_DTYPE))

    row = lax.broadcasted_iota(I32, (IDX_ROWS, LANES), 0)
    col = lax.broadcasted_iota(I32, (IDX_ROWS, LANES), 1)
    new_ok = jnp.logical_and(col <= row, col < t_new)
    keys_ref[0, n_pages] = jnp.where(new_ok, scores(knew_ref[0]), MASKED_SCORE)


def _idx_scores_call(page_table, q2, w2, knew_t, kidx_t, t_new):
    db, n_pages = page_table.shape
    n_tiles = n_pages + 1
    bspec = lambda shape: pl.BlockSpec((1,) + shape, lambda b, pt: (b,) + (0,) * len(shape))
    return pl.pallas_call(
        functools.partial(_idx_scores_kernel, n_pages=n_pages, t_new=t_new),
        grid_spec=pltpu.PrefetchScalarGridSpec(
            num_scalar_prefetch=1,
            grid=(db,),
            in_specs=[bspec((N_IDX_HEADS * IDX_ROWS, D_IDX)), bspec((N_IDX_HEADS * IDX_ROWS, LANES)),
                      bspec((D_IDX, PAGE_SIZE)), pl.BlockSpec(memory_space=pl.ANY)],
            out_specs=bspec((n_tiles, IDX_ROWS, LANES)),
            scratch_shapes=[pltpu.VMEM((2, n_pages, D_IDX, PAGE_SIZE), kidx_t.dtype),
                            pltpu.SemaphoreType.DMA((2,))]),
        out_shape=jax.ShapeDtypeStruct((db, n_tiles, IDX_ROWS, LANES), F32),
        compiler_params=pltpu.CompilerParams(dimension_semantics=("arbitrary",)),
        name="dsa_sample_scores",
    )(page_table, q2, w2, knew_t, kidx_t)


def _idx_select_kernel(keys_ref, tri_ref, bias_ref, *, n_pages, k_top, t_new):
    keys = keys_ref[...]
    n_seq, n_tiles = keys.shape[:2]

    def count_where(pred):
        part = jnp.sum(jnp.where(pred(keys), 1.0, 0.0), axis=1, keepdims=True)
        return jnp.sum(part, axis=-1, keepdims=True)

    t_row = lax.broadcasted_iota(I32, (n_seq, 1, IDX_ROWS, 1), 2)
    n_valid = (jnp.minimum(t_row, t_new - 1) + (n_pages * PAGE_SIZE + 1)).astype(F32)
    def count_ge(t):
        tv = _threshold_value(t)
        return count_where(lambda kv: kv >= tv)

    thr, _ = _select_threshold(count_ge, n_valid, float(k_top))
    need = float(k_top) - count_where(lambda kv: kv > thr)

    eq = jnp.where(keys == thr, 1.0, 0.0).astype(MXU_DTYPE)
    pref = _dot(eq.reshape(n_seq * n_tiles * IDX_ROWS, LANES), tri_ref[...]).reshape(keys.shape)
    running = jnp.zeros((n_seq, 1, IDX_ROWS, 1), F32)
    for t in range(n_tiles):
        kv = keys[:, t:t + 1]
        pt = pref[:, t:t + 1]
        take_eq = jnp.where(running + pt <= need, 0.0, NEG_BIAS)
        bias_ref[:, t:t + 1] = jnp.where(kv > thr, 0.0, jnp.where(kv == thr, take_eq, NEG_BIAS))
        running = running + pt[..., LANES - 1:LANES]


def _idx_select_call(keys, n_pages, k_top, t_new):
    db, n_tiles = keys.shape[:2]
    seqs = min(SELECT_SEQS, db)
    tri = jnp.asarray((np.arange(LANES)[:, None] <= np.arange(LANES)[None, :]).astype(np.float32), MXU_DTYPE)
    spec = pl.BlockSpec((seqs, n_tiles, IDX_ROWS, LANES), lambda b: (b, 0, 0, 0))
    return pl.pallas_call(
        functools.partial(_idx_select_kernel, n_pages=n_pages, k_top=k_top, t_new=t_new),
        grid=(db // seqs,),
        in_specs=[spec, pl.BlockSpec((LANES, LANES), lambda b: (0, 0))],
        out_specs=spec,
        out_shape=jax.ShapeDtypeStruct(keys.shape, F32),
        compiler_params=pltpu.CompilerParams(
            dimension_semantics=("arbitrary",), vmem_limit_bytes=VMEM_LIMIT),
        name="dsa_sample_select",
    )(keys, tri)


def _paged_attn_kernel(pt_ref, q_ref, bias_ref, knew_ref, vnew_ref, k_hbm, v_hbm, out_ref,
                       kbuf, vbuf, ksem, vsem, *, n_pages, t_new):
    slot = _page_gather(pt_ref, k_hbm, kbuf, ksem, n_pages)
    _page_gather(pt_ref, v_hbm, vbuf, vsem, n_pages)
    q = q_ref[0]

    def expand(b8):
        return jnp.concatenate(
            [jnp.broadcast_to(b8[t:t + 1, :], (N_HEADS_A, LANES)) for t in range(t_new)], axis=0)

    k_tiles = [kbuf[slot, p].astype(MXU_DTYPE) for p in range(n_pages)] + [knew_ref[0]]
    v_tiles = [vbuf[slot, p].astype(MXU_DTYPE) for p in range(n_pages)] + [vnew_ref[0]]
    s = jnp.concatenate([_dot(q, kt) * HEAD_DIM_A ** -0.5 + expand(bias_ref[0, n])
                         for n, kt in enumerate(k_tiles)], axis=1)
    p = jnp.exp(s - jnp.max(s, axis=-1, keepdims=True))
    pv = _dot_nt(p[:, :LANES].astype(MXU_DTYPE), v_tiles[0])
    for n, vt in enumerate(v_tiles[1:], 1):
        pv = pv + _dot_nt(p[:, n * LANES:(n + 1) * LANES].astype(MXU_DTYPE), vt)
    out_ref[0] = pv / jnp.sum(p, axis=-1, keepdims=True)


def _paged_attn_call(page_table, qp, bias, knew_t, vnew_t, k_t, v_t, t_new):
    db, n_pages = page_table.shape
    rows = t_new * N_HEADS_A
    bspec = lambda shape: pl.BlockSpec((1,) + shape, lambda b, pt: (b,) + (0,) * len(shape))
    page_buf = pltpu.VMEM((2, n_pages, KV_W, PAGE_SIZE), k_t.dtype)
    return pl.pallas_call(
        functools.partial(_paged_attn_kernel, n_pages=n_pages, t_new=t_new),
        grid_spec=pltpu.PrefetchScalarGridSpec(
            num_scalar_prefetch=1,
            grid=(db,),
            in_specs=[bspec((rows, KV_W)), bspec((n_pages + 1, IDX_ROWS, LANES)),
                      bspec((KV_W, PAGE_SIZE)), bspec((KV_W, PAGE_SIZE)),
                      pl.BlockSpec(memory_space=pl.ANY), pl.BlockSpec(memory_space=pl.ANY)],
            out_specs=bspec((rows, KV_W)),
            scratch_shapes=[page_buf, page_buf, pltpu.SemaphoreType.DMA((2,)), pltpu.SemaphoreType.DMA((2,))]),
        out_shape=jax.ShapeDtypeStruct((db, rows, KV_W), F32),
        compiler_params=pltpu.CompilerParams(dimension_semantics=("arbitrary",), vmem_limit_bytes=VMEM_LIMIT),
        name="dsa_sample_attend",
    )(page_table, qp, bias, knew_t, vnew_t, k_t, v_t)


def _layout_w_in(w_in):
    cuts = np.cumsum(PROJ_SIZES)[:-1].tolist()
    wq, wk, wv, wiq, wik, wiw, wrq, wrk, wrv, wrg = jnp.split(w_in, cuts, axis=1)
    ikw = jnp.pad(jnp.concatenate([wik, wiw], axis=1), ((0, 0), (0, LANES - D_IDX - N_IDX_HEADS)))
    w = jnp.concatenate([wq, wk, wv, wiq, ikw, wrq, wrk, wrv, wrg], axis=1)
    assert w.shape[1] == W_PROJ
    return w.astype(MXU_DTYPE)


def _rope_tables(pos):
    posf = pos.astype(F32)[:, None]

    def tables(rot_dim, theta, period, width):
        half = rot_dim // 2
        inv = theta ** (-jnp.arange(half, dtype=F32) / half)
        ang = posf * inv[None, :]
        cos, sin = jnp.cos(ang), jnp.sin(ang)
        n = pos.shape[0]
        rest = period - rot_dim
        c = jnp.concatenate([cos, cos, jnp.ones((n, rest), F32)], axis=1)
        su = jnp.concatenate([-sin, jnp.zeros((n, half + rest), F32)], axis=1)
        sd = jnp.concatenate([jnp.zeros((n, half), F32), sin, jnp.zeros((n, rest), F32)], axis=1)
        return [jnp.tile(t, (1, width // period)) for t in (c, su, sd)]

    a = tables(ROT_DIM_A, ROPE_THETA, HEAD_DIM_A, LANES)
    r = tables(DK_R, RET_THETA, DK_R, LANES)
    return jnp.concatenate(a + r, axis=1)


def kernel(x_prompt, x_sample, cache_k, cache_v, cache_kidx, state_ret, page_table, w_in, w_out, w_up, w_down,
           g_pre_mix, g_post_mix, g_pre_mlp, g_post_mlp, kidx_ln_g, kidx_ln_b, ret_gn_g, ret_gn_b):
    b, s, d = x_prompt.shape
    db, t_new, _ = x_sample.shape
    n_pages = page_table.shape[1]
    past = n_pages * PAGE_SIZE
    layer = 0
    assert w_in.shape[0] == 1 and t_new <= IDX_ROWS
    assert s % (2 * KEY_TILE) == 0 and (db * SAMPLE_ROWS) % KEY_TILE == 0

    w_proj = _layout_w_in(w_in[layer])
    wo, wu, wd = (w_[layer].astype(MXU_DTYPE) for w_ in (w_out, w_up, w_down))
    row = lambda v: v[layer][None, :].astype(F32)
    g_pre, g1, g2, g3 = row(g_pre_mix), row(g_post_mix), row(g_pre_mlp), row(g_post_mlp)
    lng = jnp.pad(row(kidx_ln_g), ((0, 0), (0, LANES - D_IDX)))
    lnb = jnp.pad(row(kidx_ln_b), ((0, 0), (0, LANES - D_IDX)))
    gng, gnb = row(ret_gn_g), row(ret_gn_b)

    tm = 2 * KEY_TILE
    pp = _proj_call(x_prompt, _rope_tables(jnp.arange(s)), g_pre, w_proj, lng, lnb, tm)
    a_out = _dsa_call(pp["qa"], pp["iq"], pp["iwT"], pp["ikd"], pp["kb"], pp["vTb"])
    zero_state = jnp.zeros((b, N_HEADS_R, DK_R, DV_R), F32)
    r_out, ret_prompt = _ret_call(pp["rq"], pp["rk"], pp["rv"], pp["rg"], zero_state, gng, gnb,
                                  RET_CHUNK, float(RET_CHUNK))
    y_prompt = _mlp_call(x_prompt.reshape(b * s, d), a_out.reshape(b * s, 512), r_out.reshape(b * s, 512),
                         wo, wu, wd, g1, g2, g3, tm).reshape(b, s, d)
    kv_out = lambda t: t.reshape(b, N_KV_A, HEAD_DIM_A, s).transpose(0, 3, 1, 2)[None]
    k_prompt, v_prompt = kv_out(pp["kT"]), kv_out(pp["vT"])
    kidx_prompt = pp["ikT"].transpose(0, 2, 1)[None]

    sr = SAMPLE_ROWS
    n_s = db * sr
    xs = jnp.pad(x_sample, ((0, 0), (0, sr - t_new), (0, 0))).reshape(1, n_s, d)
    pos_s = jnp.tile(past + jnp.arange(sr), db)
    tm_s = min(2 * KEY_TILE, n_s)
    ps = _proj_call(xs, _rope_tables(pos_s), g_pre, w_proj, lng, lnb, tm_s)
    per_seq = lambda t: t.reshape(db, sr, t.shape[-1])
    per_seq_t = lambda t: t.reshape(t.shape[1], db, sr).transpose(1, 0, 2)
    k_s, v_s, ik_s, iw_s = (per_seq_t(ps[n_]) for n_ in ("kT", "vT", "ikT", "iwT"))

    k_top = min(TOPK_MAX, (past + t_new) // 4)
    iq_s = per_seq(ps["iq"])[:, :IDX_ROWS].reshape(db, IDX_ROWS, N_IDX_HEADS, D_IDX)
    q2 = iq_s.transpose(0, 2, 1, 3).reshape(db, N_IDX_HEADS * IDX_ROWS, D_IDX)
    w2 = jnp.broadcast_to(iw_s[:, :, :IDX_ROWS].reshape(db, N_IDX_HEADS * IDX_ROWS, 1),
                          (db, N_IDX_HEADS * IDX_ROWS, LANES))
    pad_slots = lambda t: jnp.pad(t, ((0, 0), (0, 0), (0, PAGE_SIZE - sr))).astype(MXU_DTYPE)
    kidx_t = cache_kidx[layer].transpose(0, 2, 1)
    k_t = cache_k[layer].transpose(0, 2, 3, 1).reshape(-1, KV_W, PAGE_SIZE)
    v_t = cache_v[layer].transpose(0, 2, 3, 1).reshape(-1, KV_W, PAGE_SIZE)
    keys = _idx_scores_call(page_table, q2, w2, pad_slots(ik_s), kidx_t, t_new)
    bias = _idx_select_call(keys, n_pages, k_top, t_new)

    qa_s = per_seq(ps["qa"])[:, :t_new].reshape(db, t_new, N_HEADS_A, HEAD_DIM_A)
    grp = (jnp.arange(N_HEADS_A) // (N_HEADS_A // N_KV_A))[None, None, :, None, None]
    own = grp == jnp.arange(N_KV_A)[None, None, None, :, None]
    qp = jnp.where(own, qa_s[:, :, :, None, :], jnp.zeros((), MXU_DTYPE)).reshape(db, t_new * N_HEADS_A, KV_W)
    o_s = _paged_attn_call(page_table, qp, bias, pad_slots(k_s), pad_slots(v_s), k_t, v_t, t_new)
    o_s = o_s.reshape(db, t_new, N_HEADS_A, N_KV_A, HEAD_DIM_A)
    a_s = jnp.where(own, o_s, 0.0).sum(axis=3)
    a_s = jnp.pad(a_s.reshape(db, t_new, 512), ((0, 0), (0, sr - t_new), (0, 0))).astype(MXU_DTYPE)

    r_s, ret_sample = _ret_call(per_seq(ps["rq"]), per_seq(ps["rk"]), per_seq(ps["rv"]), per_seq(ps["rg"]),
                                state_ret[layer].astype(F32), gng, gnb, sr, float(t_new))
    y_s = _mlp_call(xs.reshape(n_s, d), a_s.reshape(n_s, 512), r_s.reshape(n_s, 512),
                    wo, wu, wd, g1, g2, g3, tm_s).reshape(db, sr, d)[:, :t_new]

    kv_s_out = lambda t: t[:, :, :t_new].reshape(db, N_KV_A, HEAD_DIM_A, t_new).transpose(0, 3, 1, 2)[None]
    return (y_prompt, y_s, k_prompt, v_prompt, kidx_prompt, ret_prompt[None],
            kv_s_out(k_s), kv_s_out(v_s), ik_s[:, :, :t_new].transpose(0, 2, 1)[None], ret_sample[None])
```
